```python
import jax, jax.numpy as jnp
from jax import lax
import numpy as np

D_MODEL = 1024
BATCH = 4
SEQ = 4096
DEPTH = 4
DEC_BATCH = 32
DEC_SEQ = 1
PAST_LEN = 8192
PAGE_SIZE = 128

N_MIXERS = 3
N_A = (DEPTH + 2) // N_MIXERS
N_B = (DEPTH + 1) // N_MIXERS
N_C = DEPTH // N_MIXERS
SC_WIDTH = 3
HEAD_DIM = 64
N_HEADS = D_MODEL // HEAD_DIM
Q_BLOCK = 128
FORGET_BIAS_INIT = 2.0
MASK_VALUE = -1e30
LRU_WIDTH = ((4 * D_MODEL) // 3) // 128 * 128
LRU_BLOCKS = 16
LRU_BLOCK = LRU_WIDTH // LRU_BLOCKS
LRU_CONV = 4
LRU_C = 8.0
D_FF = ((8 * D_MODEL) // 3 + 127) // 128 * 128
FFN_CONV = 3
EPS = 1e-6

kernel_name = 'fox_shortconv_rglru_hybrid_step'


def rmsnorm(x, g):
    xf = x.astype(jnp.float32)
    y = xf * lax.rsqrt(jnp.mean(xf * xf, axis=-1, keepdims=True) + EPS)
    return (y * g.astype(jnp.float32)).astype(x.dtype)


def causal_dwconv(u, buf, w, b=None):
    width = w.shape[0]
    t = u.shape[1]
    ext = jnp.concatenate([buf.astype(u.dtype), u], axis=1)
    y = w[0] * ext[:, :t]
    for k in range(1, width):
        y = y + w[k] * ext[:, k:k + t]
    if b is not None:
        y = y + b
    return y, ext[:, t:]


def short_conv_mixer(xn, buf, w_in, conv_w, w_out):
    b_gate, c_gate, h = jnp.split(xn @ w_in, 3, axis=-1)
    y, new_buf = causal_dwconv(c_gate * h, buf, conv_w)
    return (b_gate * y) @ w_out, new_buf


def fox_project(xn, w_qkv, w_f, b_f, q_norm, k_norm):
    bsz, t, _ = xn.shape
    q, k, v = jnp.split(xn @ w_qkv, 3, axis=-1)
    q = rmsnorm(q.reshape(bsz, t, N_HEADS, HEAD_DIM), q_norm)
    k = rmsnorm(k.reshape(bsz, t, N_HEADS, HEAD_DIM), k_norm)
    v = v.reshape(bsz, t, N_HEADS, HEAD_DIM)
    log_f = jax.nn.log_sigmoid((xn @ w_f + b_f).astype(jnp.float32))
    return q, k, v, log_f


def fox_attention(q, c_q, q_pos, segments):
    bsz, tq, h, dh = q.shape
    qb = Q_BLOCK if tq % Q_BLOCK == 0 else tq
    nb = tq // qb
    scale = dh ** -0.5
    q_blocks = q.reshape(bsz, nb, qb, h, dh).swapaxes(0, 1)
    cq_blocks = c_q.reshape(bsz, nb, qb, h).swapaxes(0, 1)
    pos_blocks = q_pos.reshape(nb, qb)
    sizes = [seg[0].shape[1] for seg in segments]

    def one_block(args):
        qblk, cqblk, pblk = args
        logits = []
        for k, _, c_k, k_pos in segments:
            s = jnp.einsum('bqhd,bkhd->bhqk', qblk, k).astype(jnp.float32) * scale
            s = s + cqblk.transpose(0, 2, 1)[..., None] - c_k.transpose(0, 2, 1)[:, :, None, :]
            logits.append(jnp.where(k_pos[None, :] <= pblk[:, None], s, MASK_VALUE))
        p = jax.nn.softmax(jnp.concatenate(logits, axis=-1), axis=-1)
        out = None
        offset = 0
        for (_, v, _, _), n in zip(segments, sizes):
            part = jnp.einsum('bhqk,bkhd->bqhd', p[..., offset:offset + n].astype(v.dtype), v)
            out = part if out is None else out + part
            offset += n
        return out

    out = lax.map(one_block, (q_blocks, cq_blocks, pos_blocks))
    return out.swapaxes(0, 1).reshape(bsz, tq, h * dh)


def _lru_combine(left, right):
    a_l, b_l = left
    a_r, b_r = right
    return a_l * a_r, a_r * b_l + b_r


def rglru_mixer(xn, h0, buf, w_in, b_in, conv_w, conv_b, w_a, b_a, w_i, b_i, lam, w_out):
    gate, xb = jnp.split(xn @ w_in + b_in, 2, axis=-1)
    gate = jax.nn.gelu(gate)
    xb, new_buf = causal_dwconv(xb, buf, conv_w, conv_b)
    bsz, t, _ = xb.shape
    xblk = xb.reshape(bsz, t, LRU_BLOCKS, LRU_BLOCK)
    r = jax.nn.sigmoid(jnp.einsum('btnc,ncd->btnd', xblk, w_a).reshape(bsz, t, LRU_WIDTH) + b_a)
    i = jax.nn.sigmoid(jnp.einsum('btnc,ncd->btnd', xblk, w_i).reshape(bsz, t, LRU_WIDTH) + b_i)
    log_a = -LRU_C * r.astype(jnp.float32) * jax.nn.softplus(-lam.astype(jnp.float32))
    a = jnp.exp(log_a)
    bterm = jnp.sqrt(-jnp.expm1(2.0 * log_a)) * (i * xb).astype(jnp.float32)
    bterm = bterm.at[:, 0].add(a[:, 0] * h0.astype(jnp.float32))
    _, hs = lax.associative_scan(_lru_combine, (a, bterm), axis=1)
    y = (hs.astype(xn.dtype) * gate) @ w_out
    return y, hs[:, -1].astype(xn.dtype), new_buf


def conv_ffn(xn, buf, w_gate, w_up, conv_w, conv_b, w_down):
    g, new_buf = causal_dwconv(xn @ w_gate, buf, conv_w, conv_b)
    return (jax.nn.silu(g) * (xn @ w_up)) @ w_down, new_buf


def setup_inputs(seed: int = 0) -> dict:
    key = jax.random.key(seed)
    keys = iter(jax.random.split(key, 48))

    def nrm(shape, scale=1.0):
        return jax.random.normal(next(keys), shape, jnp.float32) * scale

    n_pages = PAST_LEN // PAGE_SIZE
    n_used = DEC_BATCH * n_pages
    n_pool = n_used + n_used // 4
    d = D_MODEL

    x_prompt = nrm((BATCH, SEQ, d))
    x_sample = nrm((DEC_BATCH, DEC_SEQ, d))
    state_sconv = nrm((N_A, DEC_BATCH, SC_WIDTH - 1, d))
    cache_k = nrm((N_B, n_pool, PAGE_SIZE, N_HEADS, HEAD_DIM))
    cache_v = nrm((N_B, n_pool, PAGE_SIZE, N_HEADS, HEAD_DIM))
    cache_logf = jax.nn.log_sigmoid(FORGET_BIAS_INIT + nrm((N_B, n_pool, PAGE_SIZE, N_HEADS)))
    page_table = jax.random.permutation(next(keys), n_pool)[:n_used].reshape(DEC_BATCH, n_pages).astype(jnp.int32)
    state_lru_h = nrm((N_C, DEC_BATCH, LRU_WIDTH), 0.5)
    state_lru_conv = nrm((N_C, DEC_BATCH, LRU_CONV - 1, LRU_WIDTH))
    state_ffn_conv = nrm((DEPTH, DEC_BATCH, FFN_CONV - 1, D_FF))

    a0 = jax.random.uniform(next(keys), (N_C, LRU_WIDTH), jnp.float32, 0.9, 0.999)
    s0 = a0 ** (1.0 / LRU_C)
    lru_lambda = jnp.log(s0) - jnp.log1p(-s0)

    return {
        'x_prompt': x_prompt,
        'x_sample': x_sample,
        'state_sconv': state_sconv,
        'cache_k': cache_k,
        'cache_v': cache_v,
        'cache_logf': cache_logf,
        'page_table': page_table,
        'state_lru_h': state_lru_h,
        'state_lru_conv': state_lru_conv,
        'state_ffn_conv': state_ffn_conv,
        'mix_norm': 1.0 + nrm((DEPTH, d), 0.1),
        'ffn_norm': 1.0 + nrm((DEPTH, d), 0.1),
        'sc_w_in': nrm((N_A, d, 3 * d), d ** -0.5),
        'sc_conv_w': nrm((N_A, SC_WIDTH, d), SC_WIDTH ** -0.5),
        'sc_w_out': nrm((N_A, d, d), d ** -0.5),
        'fox_w_qkv': nrm((N_B, d, 3 * d), d ** -0.5),
        'fox_w_f': nrm((N_B, d, N_HEADS), d ** -0.5),
        'fox_b_f': FORGET_BIAS_INIT + nrm((N_B, N_HEADS), 0.1),
        'fox_q_norm': 1.0 + nrm((N_B, HEAD_DIM), 0.1),
        'fox_k_norm': 1.0 + nrm((N_B, HEAD_DIM), 0.1),
        'fox_w_o': nrm((N_B, d, d), d ** -0.5),
        'lru_w_in': nrm((N_C, d, 2 * LRU_WIDTH), d ** -0.5),
        'lru_b_in': nrm((N_C, 2 * LRU_WIDTH), 0.02),
        'lru_conv_w': nrm((N_C, LRU_CONV, LRU_WIDTH), LRU_CONV ** -0.5),
        'lru_conv_b': nrm((N_C, LRU_WIDTH), 0.02),
        'lru_w_a': nrm((N_C, LRU_BLOCKS, LRU_BLOCK, LRU_BLOCK), LRU_BLOCK ** -0.5),
        'lru_b_a': nrm((N_C, LRU_WIDTH), 0.02),
        'lru_w_i': nrm((N_C, LRU_BLOCKS, LRU_BLOCK, LRU_BLOCK), LRU_BLOCK ** -0.5),
        'lru_b_i': nrm((N_C, LRU_WIDTH), 0.02),
        'lru_lambda': lru_lambda,
        'lru_w_out': nrm((N_C, LRU_WIDTH, d), LRU_WIDTH ** -0.5),
        'ffn_w_gate': nrm((DEPTH, d, D_FF), d ** -0.5),
        'ffn_w_up': nrm((DEPTH, d, D_FF), d ** -0.5),
        'ffn_conv_w': nrm((DEPTH, FFN_CONV, D_FF), FFN_CONV ** -0.5),
        'ffn_conv_b': nrm((DEPTH, D_FF), 0.02),
        'ffn_w_down': nrm((DEPTH, D_FF, d), D_FF ** -0.5),
    }


def reference(x_prompt, x_sample, state_sconv, cache_k, cache_v, cache_logf, page_table,
              state_lru_h, state_lru_conv, state_ffn_conv,
              mix_norm, ffn_norm, sc_w_in, sc_conv_w, sc_w_out,
              fox_w_qkv, fox_w_f, fox_b_f, fox_q_norm, fox_k_norm, fox_w_o,
              lru_w_in, lru_b_in, lru_conv_w, lru_conv_b, lru_w_a, lru_b_a, lru_w_i, lru_b_i,
              lru_lambda, lru_w_out,
              ffn_w_gate, ffn_w_up, ffn_conv_w, ffn_conv_b, ffn_w_down):
    n_pages = PAST_LEN // PAGE_SIZE
    past = n_pages * PAGE_SIZE
    xp, xs = x_prompt, x_sample
    bp, tp = xp.shape[:2]
    bs, ts = xs.shape[:2]
    pos_p = jnp.arange(tp, dtype=jnp.int32)
    pos_past = jnp.arange(past, dtype=jnp.int32)
    pos_new = past + jnp.arange(ts, dtype=jnp.int32)

    sc_p, sc_s = [], []
    k_p, v_p, lf_p, k_s, v_s, lf_s = [], [], [], [], [], []
    lh_p, lh_s, lc_p, lc_s = [], [], [], []
    fc_p, fc_s = [], []

    for i in range(DEPTH):
        j = i // N_MIXERS
        hp = rmsnorm(xp, mix_norm[i])
        hs = rmsnorm(xs, mix_norm[i])
        if i % N_MIXERS == 0:
            zero_buf = jnp.zeros((bp, SC_WIDTH - 1, D_MODEL), hp.dtype)
            yp, nbp = short_conv_mixer(hp, zero_buf, sc_w_in[j], sc_conv_w[j], sc_w_out[j])
            ys, nbs = short_conv_mixer(hs, state_sconv[j], sc_w_in[j], sc_conv_w[j], sc_w_out[j])
            sc_p.append(nbp)
            sc_s.append(nbs)
        elif i % N_MIXERS == 1:
            qp, kp, vp, lfp = fox_project(hp, fox_w_qkv[j], fox_w_f[j], fox_b_f[j], fox_q_norm[j], fox_k_norm[j])
            cp = jnp.cumsum(lfp, axis=1)
            yp = fox_attention(qp, cp, pos_p, ((kp, vp, cp, pos_p),)) @ fox_w_o[j]
            qs, ks, vs, lfs = fox_project(hs, fox_w_qkv[j], fox_w_f[j], fox_b_f[j], fox_q_norm[j], fox_k_norm[j])
            k_past = cache_k[j, page_table].reshape(bs, past, N_HEADS, HEAD_DIM)
            v_past = cache_v[j, page_table].reshape(bs, past, N_HEADS, HEAD_DIM)
            lf_past = cache_logf[j, page_table].reshape(bs, past, N_HEADS).astype(jnp.float32)
            c_past = jnp.cumsum(lf_past, axis=1)
            c_new = c_past[:, -1:] + jnp.cumsum(lfs, axis=1)
            ys = fox_attention(qs, c_new, pos_new,
                               ((k_past, v_past, c_past, pos_past), (ks, vs, c_new, pos_new))) @ fox_w_o[j]
            k_p.append(kp)
            v_p.append(vp)
            lf_p.append(lfp)
            k_s.append(ks)
            v_s.append(vs)
            lf_s.append(lfs)
        else:
            h_zero = jnp.zeros((bp, LRU_WIDTH), hp.dtype)
            c_zero = jnp.zeros((bp, LRU_CONV - 1, LRU_WIDTH), hp.dtype)
            yp, nhp, ncp = rglru_mixer(hp, h_zero, c_zero, lru_w_in[j], lru_b_in[j], lru_conv_w[j], lru_conv_b[j],
                                       lru_w_a[j], lru_b_a[j], lru_w_i[j], lru_b_i[j], lru_lambda[j], lru_w_out[j])
            ys, nhs, ncs = rglru_mixer(hs, state_lru_h[j], state_lru_conv[j], lru_w_in[j], lru_b_in[j], lru_conv_w[j],
                                       lru_conv_b[j], lru_w_a[j], lru_b_a[j], lru_w_i[j], lru_b_i[j], lru_lambda[j],
                                       lru_w_out[j])
            lh_p.append(nhp)
            lh_s.append(nhs)
            lc_p.append(ncp)
            lc_s.append(ncs)
        xp = xp + yp
        xs = xs + ys

        hp = rmsnorm(xp, ffn_norm[i])
        hs = rmsnorm(xs, ffn_norm[i])
        zero_ffn = jnp.zeros((bp, FFN_CONV - 1, D_FF), hp.dtype)
        yp, nfp = conv_ffn(hp, zero_ffn, ffn_w_gate[i], ffn_w_up[i], ffn_conv_w[i], ffn_conv_b[i], ffn_w_down[i])
        ys, nfs = conv_ffn(hs, state_ffn_conv[i], ffn_w_gate[i], ffn_w_up[i], ffn_conv_w[i], ffn_conv_b[i], ffn_w_down[i])
        fc_p.append(nfp)
        fc_s.append(nfs)
        xp = xp + yp
        xs = xs + ys

    new_sconv_prompt = jnp.stack(sc_p)
    new_sconv_sample = jnp.stack(sc_s)
    new_k_prompt = jnp.stack(k_p)
    new_v_prompt = jnp.stack(v_p)
    new_logf_prompt = jnp.stack(lf_p)
    new_k_sample = jnp.stack(k_s)
    new_v_sample = jnp.stack(v_s)
    new_logf_sample = jnp.stack(lf_s)
    new_lru_h_prompt = jnp.stack(lh_p)
    new_lru_h_sample = jnp.stack(lh_s)
    new_lru_conv_prompt = jnp.stack(lc_p)
    new_lru_conv_sample = jnp.stack(lc_s)
    new_ffn_conv_prompt = jnp.stack(fc_p)
    new_ffn_conv_sample = jnp.stack(fc_s)
    return (xp, xs,
            new_sconv_prompt, new_sconv_sample,
            new_k_prompt, new_v_prompt, new_logf_prompt,
            new_k_sample, new_v_sample, new_logf_sample,
            new_lru_h_prompt, new_lru_h_sample,
            new_lru_conv_prompt, new_lru_conv_sample,
            new_ffn_conv_prompt, new_ffn_conv_sample)
```

```python
import functools

import jax
import jax.numpy as jnp
from jax import lax
from jax.experimental import pallas as pl
from jax.experimental.pallas import tpu as pltpu

F32 = jnp.float32
BF16 = jnp.bfloat16

EPS = 1e-6
HEAD_DIM = 64
LRU_C = 8.0
MASK_VALUE = -1e30

SUBLANES = 8
LANES = 128
V7X_VMEM_LIMIT_BYTES = 56 * 1024 * 1024
ROW_TILE = 512
LRU_ROW_TILE = 256
ATTN_TILE = 512
DECODE_PAGES_PER_STEP = 4
LOGF_PAGES_PER_STEP = 8


def _dot(a, b):
    return jnp.dot(a, b, preferred_element_type=F32)


def _dot_nt(a, b):
    return lax.dot_general(a, b, (((1,), (1,)), ((), ())), preferred_element_type=F32)


def _split2(x):
    hi = x.astype(BF16)
    lo = (x - hi.astype(F32)).astype(BF16)
    return hi, lo


def _split3(x):
    hi = x.astype(BF16)
    r = x - hi.astype(F32)
    mid = r.astype(BF16)
    lo = (r - mid.astype(F32)).astype(BF16)
    return hi, mid, lo


def _dot_exact_rhs01(x_parts, m01):
    out = _dot(x_parts[0], m01)
    for p in x_parts[1:]:
        out = out + _dot(p, m01)
    return out


def _rmsnorm(x, g):
    ms = jnp.mean(x * x, axis=-1, keepdims=True)
    return x * lax.rsqrt(ms + EPS) * g


def _sigmoid(x):
    return 1.0 / (1.0 + jnp.exp(-x))


def _softplus(x):
    return jnp.maximum(x, 0.0) + jnp.log1p(jnp.exp(-jnp.abs(x)))


def _gelu_tanh(x):
    c = 0.7978845608028654
    return x * (0.5 * (1.0 + jnp.tanh(c * (x + 0.044715 * (x * x * x)))))


def _resident(shape):
    n = len(shape)
    return pl.BlockSpec(shape, lambda *_: (0,) * n, pipeline_mode=pl.Buffered(1))


def _params(*sem):
    return pltpu.CompilerParams(dimension_semantics=sem, vmem_limit_bytes=V7X_VMEM_LIMIT_BYTES)


def _row_tile(t, pref):
    return pref if t % pref == 0 else t


def _ffn_chunks(f):
    for n in (2, 1):
        if f % (n * LANES) == 0:
            return n
    return 1


def _head_rmsnorm(x, gain_tiled, seg, expand):
    ss = _dot_exact_rhs01(_split2(x * x), seg)
    inv = lax.rsqrt(ss * (1.0 / HEAD_DIM) + EPS)
    return x * _dot_exact_rhs01(_split3(inv), expand) * gain_tiled


def _log_sigmoid(x):
    return -_softplus(-x)


def _lru_gates(y, wai_ref, ba, bi, lam, w):
    yb = y.astype(BF16)
    r = _sigmoid(_dot(yb, wai_ref[:, 0:w]) + ba)
    i = _sigmoid(_dot(yb, wai_ref[:, w:2 * w]) + bi)
    log_a = (-LRU_C) * r * _softplus(-lam)
    a = jnp.exp(log_a)
    b = jnp.sqrt(1.0 - jnp.exp(2.0 * log_a)) * (i * y)
    return a, b


def _sconv_pre_kernel(x_ref, g_ref, win_ref, cw_ref, z_ref, nb_ref, ubuf):
    tm, d = x_ref.shape[1], x_ref.shape[2]

    @pl.when(pl.program_id(1) == 0)
    def _():
        ubuf[0:SUBLANES, :] = jnp.zeros((SUBLANES, d), F32)

    xn = _rmsnorm(x_ref[0], g_ref[...]).astype(BF16)
    bg = _dot(xn, win_ref[:, 0:d])
    u = _dot(xn, win_ref[:, d:2 * d]) * _dot(xn, win_ref[:, 2 * d:3 * d])
    ubuf[SUBLANES:SUBLANES + tm, :] = u
    y = (cw_ref[0:1, :] * ubuf[SUBLANES - 2:SUBLANES - 2 + tm, :]
         + cw_ref[1:2, :] * ubuf[SUBLANES - 1:SUBLANES - 1 + tm, :]
         + cw_ref[2:3, :] * u)
    z_ref[0] = (bg * y).astype(BF16)
    nb_ref[0] = ubuf[SUBLANES + tm - 2:SUBLANES + tm, :]
    ubuf[0:SUBLANES, :] = ubuf[tm:tm + SUBLANES, :]


def _sconv_pre(x, g, w_in, conv_w):
    b, t, d = x.shape
    tm = _row_tile(t, ROW_TILE)
    return pl.pallas_call(
        _sconv_pre_kernel,
        grid=(b, t // tm),
        in_specs=[
            pl.BlockSpec((1, tm, d), lambda i, j: (i, j, 0)),
            _resident((1, d)),
            _resident((d, 3 * d)),
            _resident((conv_w.shape[0], d)),
        ],
        out_specs=[
            pl.BlockSpec((1, tm, d), lambda i, j: (i, j, 0)),
            pl.BlockSpec((1, 2, d), lambda i, j: (i, 0, 0)),
        ],
        out_shape=[
            jax.ShapeDtypeStruct((b, t, d), BF16),
            jax.ShapeDtypeStruct((b, 2, d), F32),
        ],
        scratch_shapes=[pltpu.VMEM((SUBLANES + tm, d), F32)],
        compiler_params=_params("arbitrary", "arbitrary"),
        name="sconv_pre",
    )(x, g, w_in, conv_w)


def _ffn_body(x1, xn, gprev_fn, gstore_fn, wg_ref, wu_ref, cw_ref, cb_ref, wd_ref):
    f = wg_ref.shape[1]
    n_chunks = _ffn_chunks(f)
    fc = f // n_chunks
    acc = x1
    for c in range(n_chunks):
        sl = slice(c * fc, (c + 1) * fc)
        g = _dot(xn, wg_ref[:, sl])
        gstore_fn(sl, g)
        gc = (cw_ref[0:1, sl] * gprev_fn(sl, 2) + cw_ref[1:2, sl] * gprev_fn(sl, 1)
              + cw_ref[2:3, sl] * g + cb_ref[:, sl])
        up = _dot(xn, wu_ref[:, sl])
        hmid = (gc * _sigmoid(gc) * up).astype(BF16)
        acc = acc + _dot(hmid, wd_ref[sl, :])
    return acc


def _post_kernel(x_ref, z_ref, wo_ref, g_ref, wg_ref, wu_ref, cw_ref, cb_ref, wd_ref,
                 o_ref, nb_ref, gbuf):
    tm = x_ref.shape[1]
    f = wg_ref.shape[1]

    @pl.when(pl.program_id(1) == 0)
    def _():
        gbuf[0:SUBLANES, :] = jnp.zeros((SUBLANES, f), F32)

    x1 = x_ref[0] + _dot(z_ref[0], wo_ref[...])
    xn = _rmsnorm(x1, g_ref[...]).astype(BF16)

    def gstore(sl, g):
        gbuf[SUBLANES:SUBLANES + tm, sl] = g

    def gprev(sl, k):
        return gbuf[SUBLANES - k:SUBLANES - k + tm, sl]

    o_ref[0] = _ffn_body(x1, xn, gprev, gstore, wg_ref, wu_ref, cw_ref, cb_ref, wd_ref)
    nb_ref[0] = gbuf[SUBLANES + tm - 2:SUBLANES + tm, :]
    gbuf[0:SUBLANES, :] = gbuf[tm:tm + SUBLANES, :]


def _post(x, z, w_out, g, w_gate, w_up, conv_w, conv_b, w_down):
    b, t, d = x.shape
    kz = z.shape[2]
    f = w_gate.shape[1]
    tm = _row_tile(t, ROW_TILE)
    return pl.pallas_call(
        _post_kernel,
        grid=(b, t // tm),
        in_specs=[
            pl.BlockSpec((1, tm, d), lambda i, j: (i, j, 0)),
            pl.BlockSpec((1, tm, kz), lambda i, j: (i, j, 0)),
            _resident((kz, d)),
            _resident((1, d)),
            _resident((d, f)),
            _resident((d, f)),
            _resident((conv_w.shape[0], f)),
            _resident((1, f)),
            _resident((f, d)),
        ],
        out_specs=[
            pl.BlockSpec((1, tm, d), lambda i, j: (i, j, 0)),
            pl.BlockSpec((1, 2, f), lambda i, j: (i, 0, 0)),
        ],
        out_shape=[
            jax.ShapeDtypeStruct((b, t, d), F32),
            jax.ShapeDtypeStruct((b, 2, f), F32),
        ],
        scratch_shapes=[pltpu.VMEM((SUBLANES + tm, f), F32)],
        compiler_params=_params("arbitrary", "arbitrary"),
        name="post_ffn",
    )(x, z, w_out, g, w_gate, w_up, conv_w, conv_b, w_down)


def _fox_project(xn, wqkv_ref, wf_ref, bf_ref, qn_ref, kn_ref, seg_ref, exp_ref):
    d = xn.shape[1]
    q = _head_rmsnorm(_dot(xn, wqkv_ref[:, 0:d]), qn_ref[...], seg_ref[...], exp_ref[...])
    k = _head_rmsnorm(_dot(xn, wqkv_ref[:, d:2 * d]), kn_ref[...], seg_ref[...], exp_ref[...])
    v = _dot(xn, wqkv_ref[:, 2 * d:3 * d])
    lf = _log_sigmoid(_dot(xn, wf_ref[...]) + bf_ref[...])
    return q, k, v, lf


def _fox_pre_kernel(x_ref, g_ref, wqkv_ref, wf_ref, bf_ref, qn_ref, kn_ref, seg_ref, exp_ref,
                    tri_ref, q4_ref, k4_ref, v4_ref, k_ref, v_ref, lf_ref, c_ref, carry):
    tm = x_ref.shape[1]
    n_heads = lf_ref.shape[2]

    @pl.when(pl.program_id(1) == 0)
    def _():
        carry[...] = jnp.zeros_like(carry)

    xn = _rmsnorm(x_ref[0], g_ref[...]).astype(BF16)
    q, k, v, lf = _fox_project(xn, wqkv_ref, wf_ref, bf_ref, qn_ref, kn_ref, seg_ref, exp_ref)
    k_ref[0] = k
    v_ref[0] = v
    qs = q * (HEAD_DIM ** -0.5)
    for h in range(n_heads):
        sl = slice(h * HEAD_DIM, (h + 1) * HEAD_DIM)
        q4_ref[0, h] = qs[:, sl].astype(BF16)
        k4_ref[0, h] = k[:, sl].astype(BF16)
        v4_ref[0, h] = v[:, sl].astype(BF16)
    lf_ref[0] = lf[:, 0:n_heads]
    c = _dot_exact_rhs01_lhs(tri_ref[...], _split3(lf)) + carry[0:1, :]
    c_ref[0] = c[:, 0:n_heads]
    carry[0:1, :] = c[tm - 1:tm, :]


def _dot_exact_rhs01_lhs(m01, x_parts):
    out = _dot(m01, x_parts[0])
    for p in x_parts[1:]:
        out = out + _dot(m01, p)
    return out


def _fox_pre(x, g, w_qkv, w_f, b_f, qn, kn, seg, expand, tri, n_heads):
    b, t, d = x.shape
    tm = tri.shape[0]
    row = lambda i, j: (i, j, 0)
    head = lambda i, j: (i, 0, j, 0)
    return pl.pallas_call(
        _fox_pre_kernel,
        grid=(b, t // tm),
        in_specs=[
            pl.BlockSpec((1, tm, d), row),
            _resident((1, d)),
            _resident((d, 3 * d)),
            _resident((d, LANES)),
            _resident((1, LANES)),
            _resident((1, d)),
            _resident((1, d)),
            _resident((d, LANES)),
            _resident((LANES, d)),
            _resident((tm, tm)),
        ],
        out_specs=[
            pl.BlockSpec((1, n_heads, tm, HEAD_DIM), head),
            pl.BlockSpec((1, n_heads, tm, HEAD_DIM), head),
            pl.BlockSpec((1, n_heads, tm, HEAD_DIM), head),
            pl.BlockSpec((1, tm, d), row),
            pl.BlockSpec((1, tm, d), row),
            pl.BlockSpec((1, tm, n_heads), row),
            pl.BlockSpec((1, tm, n_heads), row),
        ],
        out_shape=[
            jax.ShapeDtypeStruct((b, n_heads, t, HEAD_DIM), BF16),
            jax.ShapeDtypeStruct((b, n_heads, t, HEAD_DIM), BF16),
            jax.ShapeDtypeStruct((b, n_heads, t, HEAD_DIM), BF16),
            jax.ShapeDtypeStruct((b, t, d), F32),
            jax.ShapeDtypeStruct((b, t, d), F32),
            jax.ShapeDtypeStruct((b, t, n_heads), F32),
            jax.ShapeDtypeStruct((b, t, n_heads), F32),
        ],
        scratch_shapes=[pltpu.VMEM((SUBLANES, LANES), F32)],
        compiler_params=_params("arbitrary", "arbitrary"),
        name="fox_pre",
    )(x, g, w_qkv, w_f, b_f, qn, kn, seg, expand, tri)


def _fox_attn_kernel(q_ref, k_ref, v_ref, ct_ref, o_ref, m_ref, l_ref, acc_ref):
    tq = q_ref.shape[2]
    qi = pl.program_id(2)
    rows = lax.broadcasted_iota(jnp.int32, (tq, tq), 0)
    cols = lax.broadcasted_iota(jnp.int32, (tq, tq), 1)

    for j in range(q_ref.shape[1]):
        q = q_ref[0, j]
        m_ref[...] = jnp.full(m_ref.shape, MASK_VALUE, F32)
        l_ref[...] = jnp.zeros(l_ref.shape, F32)
        acc_ref[...] = jnp.zeros(acc_ref.shape, F32)

        def block(ki, masked):
            start = pl.multiple_of(ki * tq, tq)
            k = k_ref[0, j, pl.ds(start, tq), :]
            v = v_ref[0, j, pl.ds(start, tq), :]
            s = _dot_nt(q, k) - ct_ref[0, j, pl.ds(ki, 1), :]
            if masked:
                s = jnp.where(cols <= rows, s, MASK_VALUE)
            m_prev = m_ref[...]
            m_new = jnp.maximum(m_prev, jnp.max(s, axis=1, keepdims=True))
            alpha = jnp.exp(m_prev - m_new)
            p = jnp.exp(s - m_new)
            l_ref[...] = alpha * l_ref[...] + jnp.sum(p, axis=1, keepdims=True)
            acc_ref[...] = alpha * acc_ref[...] + _dot(p.astype(BF16), v)
            m_ref[...] = m_new

        def body(ki, carry):
            block(ki, False)
            return carry

        lax.fori_loop(0, qi, body, 0)
        block(qi, True)
        o_ref[0, :, j * HEAD_DIM:(j + 1) * HEAD_DIM] = (acc_ref[...] / l_ref[...]).astype(BF16)


def _fox_attn(q4, k4, v4, ct):
    b, n_heads, t, _ = q4.shape
    tq = _row_tile(t, ATTN_TILE)
    pair = 2
    return pl.pallas_call(
        _fox_attn_kernel,
        grid=(b, n_heads // pair, t // tq),
        in_specs=[
            pl.BlockSpec((1, pair, tq, HEAD_DIM), lambda i, h, j: (i, h, j, 0)),
            pl.BlockSpec((1, pair, t, HEAD_DIM), lambda i, h, j: (i, h, 0, 0)),
            pl.BlockSpec((1, pair, t, HEAD_DIM), lambda i, h, j: (i, h, 0, 0)),
            pl.BlockSpec((1, pair, t // tq, tq), lambda i, h, j: (i, h, 0, 0)),
        ],
        out_specs=pl.BlockSpec((1, tq, pair * HEAD_DIM), lambda i, h, j: (i, j, h)),
        out_shape=jax.ShapeDtypeStruct((b, t, n_heads * HEAD_DIM), BF16),
        scratch_shapes=[
            pltpu.VMEM((tq, 1), F32),
            pltpu.VMEM((tq, 1), F32),
            pltpu.VMEM((tq, HEAD_DIM), F32),
        ],
        compiler_params=_params("arbitrary", "arbitrary", "arbitrary"),
        name="fox_attn",
    )(q4, k4, v4, ct)


def _shift_rows(x, s, fill, row_idx):
    return jnp.where(row_idx >= s, pltpu.roll(x, s, 0), fill)


def _lru_pre_kernel(x_ref, g_ref, win_ref, bin_ref, cw_ref, cb_ref, wai_ref, ba_ref, bi_ref,
                    lam_ref, z_ref, nh_ref, nc_ref, xbuf, hcarry):
    tm = x_ref.shape[1]
    w = cw_ref.shape[1]
    taps = cw_ref.shape[0]

    @pl.when(pl.program_id(1) == 0)
    def _():
        xbuf[0:SUBLANES, :] = jnp.zeros((SUBLANES, w), F32)
        hcarry[...] = jnp.zeros_like(hcarry)

    xn = _rmsnorm(x_ref[0], g_ref[...]).astype(BF16)
    gate = _gelu_tanh(_dot(xn, win_ref[:, 0:w]) + bin_ref[:, 0:w])
    xb = _dot(xn, win_ref[:, w:2 * w]) + bin_ref[:, w:2 * w]
    xbuf[SUBLANES:SUBLANES + tm, :] = xb
    y = cw_ref[taps - 1:taps, :] * xb + cb_ref[...]
    for k in range(1, taps):
        y = y + cw_ref[taps - 1 - k:taps - k, :] * xbuf[SUBLANES - k:SUBLANES - k + tm, :]
    nc_ref[0] = xbuf[SUBLANES + tm - (taps - 1):SUBLANES + tm, :]
    xbuf[0:SUBLANES, :] = xbuf[tm:tm + SUBLANES, :]

    a, bt = _lru_gates(y, wai_ref, ba_ref[...], bi_ref[...], lam_ref[...], w)
    row_idx = lax.broadcasted_iota(jnp.int32, (tm, w), 0)
    s = 1
    while s < tm:
        bt = a * _shift_rows(bt, s, 0.0, row_idx) + bt
        a = a * _shift_rows(a, s, 1.0, row_idx)
        s *= 2
    h = a * hcarry[0:1, :] + bt
    hcarry[0:1, :] = h[tm - 1:tm, :]
    nh_ref[0] = h[tm - 1:tm, :]
    z_ref[0] = (h * gate).astype(BF16)


def _lru_pre(x, g, w_in, b_in, conv_w, conv_b, w_ai, b_a, b_i, lam):
    b, t, d = x.shape
    w = conv_w.shape[1]
    taps = conv_w.shape[0]
    tm = _row_tile(t, LRU_ROW_TILE)
    return pl.pallas_call(
        _lru_pre_kernel,
        grid=(b, t // tm),
        in_specs=[
            pl.BlockSpec((1, tm, d), lambda i, j: (i, j, 0)),
            _resident((1, d)),
            _resident((d, 2 * w)),
            _resident((1, 2 * w)),
            _resident((taps, w)),
            _resident((1, w)),
            _resident((w, 2 * w)),
            _resident((1, w)),
            _resident((1, w)),
            _resident((1, w)),
        ],
        out_specs=[
            pl.BlockSpec((1, tm, w), lambda i, j: (i, j, 0)),
            pl.BlockSpec((1, 1, w), lambda i, j: (i, 0, 0)),
            pl.BlockSpec((1, taps - 1, w), lambda i, j: (i, 0, 0)),
        ],
        out_shape=[
            jax.ShapeDtypeStruct((b, t, w), BF16),
            jax.ShapeDtypeStruct((b, 1, w), F32),
            jax.ShapeDtypeStruct((b, taps - 1, w), F32),
        ],
        scratch_shapes=[
            pltpu.VMEM((SUBLANES + tm, w), F32),
            pltpu.VMEM((SUBLANES, w), F32),
        ],
        compiler_params=_params("arbitrary", "arbitrary"),
        name="lru_pre",
    )(x, g, w_in, b_in, conv_w, conv_b, w_ai, b_a, b_i, lam)


def _whole(shape):
    n = len(shape)
    return pl.BlockSpec(shape, lambda *_: (0,) * n)


def _single_step_call(kernel, inputs, out_shapes, name):
    return pl.pallas_call(
        kernel,
        grid=(1,),
        in_specs=[_resident(a.shape) for a in inputs],
        out_specs=[_whole(s.shape) for s in out_shapes],
        out_shape=out_shapes,
        compiler_params=_params("arbitrary"),
        name=name,
    )(*inputs)


def _sconv_s_kernel(x_ref, g_ref, win_ref, cw_ref, b0_ref, b1_ref, z_ref, u_ref):
    d = x_ref.shape[1]
    xn = _rmsnorm(x_ref[...], g_ref[...]).astype(BF16)
    bg = _dot(xn, win_ref[:, 0:d])
    u = _dot(xn, win_ref[:, d:2 * d]) * _dot(xn, win_ref[:, 2 * d:3 * d])
    y = cw_ref[0:1, :] * b0_ref[...] + cw_ref[1:2, :] * b1_ref[...] + cw_ref[2:3, :] * u
    z_ref[...] = (bg * y).astype(BF16)
    u_ref[...] = u


def _post_s_kernel(x_ref, z_ref, wo_ref, g_ref, wg_ref, wu_ref, cw_ref, cb_ref, wd_ref,
                   b0_ref, b1_ref, o_ref, gn_ref):
    x1 = x_ref[...] + _dot(z_ref[...], wo_ref[...])
    xn = _rmsnorm(x1, g_ref[...]).astype(BF16)

    def gstore(sl, g):
        gn_ref[:, sl] = g

    def gprev(sl, k):
        return (b1_ref if k == 1 else b0_ref)[:, sl]

    o_ref[...] = _ffn_body(x1, xn, gprev, gstore, wg_ref, wu_ref, cw_ref, cb_ref, wd_ref)


def _fox_pre_s_kernel(x_ref, g_ref, wqkv_ref, wf_ref, bf_ref, qn_ref, kn_ref, seg_ref, exp_ref,
                      q_ref, k_ref, v_ref, lf_ref):
    xn = _rmsnorm(x_ref[...], g_ref[...]).astype(BF16)
    q, k, v, lf = _fox_project(xn, wqkv_ref, wf_ref, bf_ref, qn_ref, kn_ref, seg_ref, exp_ref)
    q_ref[...] = q * (HEAD_DIM ** -0.5)
    k_ref[...] = k
    v_ref[...] = v
    lf_ref[...] = lf


def _lru_s_kernel(x_ref, g_ref, win_ref, bin_ref, cw_ref, cb_ref, wai_ref, ba_ref, bi_ref, lam_ref,
                  h0_ref, c0_ref, c1_ref, c2_ref, z_ref, h_ref, xb_ref):
    w = cw_ref.shape[1]
    xn = _rmsnorm(x_ref[...], g_ref[...]).astype(BF16)
    gate = _gelu_tanh(_dot(xn, win_ref[:, 0:w]) + bin_ref[:, 0:w])
    xb = _dot(xn, win_ref[:, w:2 * w]) + bin_ref[:, w:2 * w]
    y = (cw_ref[0:1, :] * c0_ref[...] + cw_ref[1:2, :] * c1_ref[...] + cw_ref[2:3, :] * c2_ref[...]
         + cw_ref[3:4, :] * xb + cb_ref[...])
    a, bt = _lru_gates(y, wai_ref, ba_ref[...], bi_ref[...], lam_ref[...], w)
    h = a * h0_ref[...] + bt
    h_ref[...] = h
    xb_ref[...] = xb
    z_ref[...] = (h * gate).astype(BF16)


def _logf_suffix_kernel(pt_ref, *refs, n_slots):
    lf_refs = refs[:n_slots]
    lfn_ref, upper_ref, out_ref, carry = refs[n_slots:]

    @pl.when(pl.program_id(1) == 0)
    def _():
        carry[...] = jnp.zeros_like(carry)

    for i in reversed(range(n_slots)):
        lf = lf_refs[i][...]
        within = _dot_exact_rhs01_lhs(upper_ref[...], _split3(lf))
        out_ref[0, i] = within + carry[0:1, :] + lfn_ref[0]
        carry[0:1, :] = carry[0:1, :] + jnp.sum(lf, axis=0, keepdims=True)


def _logf_suffix(cache_logf, layer, page_table, lf_new, upper):
    bs, n_pages = page_table.shape
    page, n_heads = cache_logf.shape[2], cache_logf.shape[3]
    g = LOGF_PAGES_PER_STEP if n_pages % LOGF_PAGES_PER_STEP == 0 else 1
    n_groups = n_pages // g

    def page_spec(i):
        def idx(b, p, pt):
            return (layer, pt[b, (n_groups - 1 - p) * g + i], 0, 0)
        return pl.BlockSpec((None, None, page, n_heads), idx)

    grid_spec = pltpu.PrefetchScalarGridSpec(
        num_scalar_prefetch=1,
        grid=(bs, n_groups),
        in_specs=[page_spec(i) for i in range(g)] + [
            pl.BlockSpec((1, 1, n_heads), lambda b, p, pt: (b, 0, 0)),
            pl.BlockSpec((page, page), lambda b, p, pt: (0, 0)),
        ],
        out_specs=pl.BlockSpec((1, g, page, n_heads), lambda b, p, pt: (b, n_groups - 1 - p, 0, 0)),
        scratch_shapes=[pltpu.VMEM((SUBLANES, n_heads), F32)],
    )
    return pl.pallas_call(
        functools.partial(_logf_suffix_kernel, n_slots=g),
        grid_spec=grid_spec,
        out_shape=jax.ShapeDtypeStruct((bs, n_pages, page, n_heads), F32),
        compiler_params=_params("arbitrary", "arbitrary"),
        name="logf_suffix",
    )(page_table, *([cache_logf] * g), lf_new, upper)


def _decode_attn_kernel(pt_ref, *refs, n_slots):
    k_refs = refs[:n_slots]
    v_refs = refs[n_slots:2 * n_slots]
    b_refs = refs[2 * n_slots:3 * n_slots]
    q_ref, kn_ref, vn_ref, o_ref, m_ref, l_ref, acc_ref = refs[3 * n_slots:]
    n_heads = q_ref.shape[1]
    q = q_ref[0]

    @pl.when(pl.program_id(1) == 0)
    def _():
        s_new = jnp.sum(q.astype(F32) * kn_ref[0].astype(BF16).astype(F32), axis=1, keepdims=True)
        m_ref[...] = s_new
        l_ref[...] = jnp.ones_like(l_ref)
        acc_ref[...] = vn_ref[0].astype(BF16).astype(F32)

    for i in range(n_slots):
        page = k_refs[i].shape[0]
        rows = page * n_heads
        k2 = k_refs[i][...].reshape(rows, HEAD_DIM).astype(BF16)
        v2 = v_refs[i][...].reshape(rows, HEAD_DIM).astype(BF16)
        head_of_col = lax.broadcasted_iota(jnp.int32, (n_heads, rows), 1) % n_heads
        own = head_of_col == lax.broadcasted_iota(jnp.int32, (n_heads, rows), 0)
        s = jnp.where(own, _dot_nt(q, k2) + b_refs[i][0, 0], MASK_VALUE)
        m_prev = m_ref[...]
        m_new = jnp.maximum(m_prev, jnp.max(s, axis=1, keepdims=True))
        alpha = jnp.exp(m_prev - m_new)
        p = jnp.exp(s - m_new)
        l_ref[...] = alpha * l_ref[...] + jnp.sum(p, axis=1, keepdims=True)
        acc_ref[...] = alpha * acc_ref[...] + _dot(p.astype(BF16), v2)
        m_ref[...] = m_new

    @pl.when(pl.program_id(1) == pl.num_programs(1) - 1)
    def _():
        o_ref[0] = acc_ref[...] / l_ref[...]


def _decode_attn(cache_k, cache_v, layer, page_table, bias, q3, k3, v3):
    bs, n_pages = page_table.shape
    page, n_heads = cache_k.shape[2], cache_k.shape[3]
    g = DECODE_PAGES_PER_STEP if n_pages % DECODE_PAGES_PER_STEP == 0 else 1
    n_groups = n_pages // g

    def page_spec(i):
        def idx(b, p, pt):
            return (layer, pt[b, p * g + i], 0, 0, 0)
        return pl.BlockSpec((None, None, page, n_heads, HEAD_DIM), idx)

    def bias_spec(i):
        return pl.BlockSpec((1, 1, 1, page * n_heads), lambda b, p, pt: (b, p * g + i, 0, 0))

    per_seq = pl.BlockSpec((1, n_heads, HEAD_DIM), lambda b, p, pt: (b, 0, 0))
    grid_spec = pltpu.PrefetchScalarGridSpec(
        num_scalar_prefetch=1,
        grid=(bs, n_groups),
        in_specs=([page_spec(i) for i in range(g)] + [page_spec(i) for i in range(g)]
                  + [bias_spec(i) for i in range(g)] + [per_seq, per_seq, per_seq]),
        out_specs=per_seq,
        scratch_shapes=[
            pltpu.VMEM((n_heads, 1), F32),
            pltpu.VMEM((n_heads, 1), F32),
            pltpu.VMEM((n_heads, HEAD_DIM), F32),
        ],
    )
    return pl.pallas_call(
        functools.partial(_decode_attn_kernel, n_slots=g),
        grid_spec=grid_spec,
        out_shape=jax.ShapeDtypeStruct((bs, n_heads, HEAD_DIM), F32),
        compiler_params=_params("arbitrary", "arbitrary"),
        name="decode_attn",
    )(page_table, *([cache_k] * g), *([cache_v] * g), *([bias] * g), q3, k3, v3)


def _block_diag(w):
    n, c, _ = w.shape
    eye = jnp.eye(n, dtype=w.dtype)
    return (eye[:, None, :, None] * w[:, :, None, :]).reshape(n * c, n * c)


def kernel(x_prompt, x_sample, state_sconv, cache_k, cache_v, cache_logf, page_table, state_lru_h, state_lru_conv, state_ffn_conv, mix_norm, ffn_norm, sc_w_in, sc_conv_w, sc_w_out, fox_w_qkv, fox_w_f, fox_b_f, fox_q_norm, fox_k_norm, fox_w_o, lru_w_in, lru_b_in, lru_conv_w, lru_conv_b, lru_w_a, lru_b_a, lru_w_i, lru_b_i, lru_lambda, lru_w_out, ffn_w_gate, ffn_w_up, ffn_conv_w, ffn_conv_b, ffn_w_down):
    depth, d = mix_norm.shape
    bp, tp, _ = x_prompt.shape
    bs, ts, _ = x_sample.shape
    assert ts == 1, "the sample group carries one new row per sequence"
    n_heads = fox_w_f.shape[-1]
    assert n_heads * HEAD_DIM == d and n_heads <= LANES
    n_mixers = 3
    f = ffn_w_gate.shape[-1]
    page = cache_k.shape[2]

    xp = x_prompt
    xs = x_sample.reshape(bs, d)
    row2 = lambda v: v.reshape(1, -1)

    lane_head = jnp.arange(d, dtype=jnp.int32) // HEAD_DIM
    seg = (lane_head[:, None] == jnp.arange(LANES, dtype=jnp.int32)[None, :]).astype(BF16)
    expand = seg.T
    tm_fox = _row_tile(tp, ROW_TILE)
    tri = (jnp.arange(tm_fox)[:, None] >= jnp.arange(tm_fox)[None, :]).astype(BF16)
    upper = (jnp.arange(page)[:, None] < jnp.arange(page)[None, :]).astype(BF16)

    sc_p, sc_s, fc_p, fc_s = [], [], [], []
    k_p, v_p, lf_p, k_s, v_s, lf_s = [], [], [], [], [], []
    lh_p, lh_s, lc_p, lc_s = [], [], [], []

    for i in range(depth):
        j = i // n_mixers
        g_mix = row2(mix_norm[i])
        if i % n_mixers == 0:
            w_in = sc_w_in[j].astype(BF16)
            w_out = sc_w_out[j].astype(BF16)
            zp, nbp = _sconv_pre(xp, g_mix, w_in, sc_conv_w[j])
            zs, us = _single_step_call(
                _sconv_s_kernel,
                [xs, g_mix, w_in, sc_conv_w[j], state_sconv[j, :, 0], state_sconv[j, :, 1]],
                [jax.ShapeDtypeStruct((bs, d), BF16), jax.ShapeDtypeStruct((bs, d), F32)],
                "sconv_sample")
            sc_p.append(nbp)
            sc_s.append(jnp.stack([state_sconv[j, :, 1], us], axis=1))
        elif i % n_mixers == 1:
            w_qkv = fox_w_qkv[j].astype(BF16)
            w_f = jnp.pad(fox_w_f[j], ((0, 0), (0, LANES - n_heads))).astype(BF16)
            b_f = jnp.pad(fox_b_f[j], (0, LANES - n_heads)).reshape(1, LANES)
            qn = row2(jnp.tile(fox_q_norm[j], n_heads))
            kn = row2(jnp.tile(fox_k_norm[j], n_heads))
            w_out = fox_w_o[j].astype(BF16)
            q4, k4, v4, kp, vp, lfp, cp = _fox_pre(xp, g_mix, w_qkv, w_f, b_f, qn, kn, seg, expand, tri, n_heads)
            tq = _row_tile(tp, ATTN_TILE)
            ct = jnp.swapaxes(cp, 1, 2).reshape(bp, n_heads, tp // tq, tq)
            zp = _fox_attn(q4, k4, v4, ct)
            k_p.append(kp.reshape(bp, tp, n_heads, HEAD_DIM))
            v_p.append(vp.reshape(bp, tp, n_heads, HEAD_DIM))
            lf_p.append(lfp)

            qs, ks, vs, lfs = _single_step_call(
                _fox_pre_s_kernel,
                [xs, g_mix, w_qkv, w_f, b_f, qn, kn, seg, expand],
                [jax.ShapeDtypeStruct((bs, d), F32)] * 3 + [jax.ShapeDtypeStruct((bs, LANES), F32)],
                "fox_pre_sample")
            lfs = lfs[:, :n_heads]
            bias = _logf_suffix(cache_logf, j, page_table, lfs.reshape(bs, 1, n_heads), upper)
            bias = bias.reshape(bs, page_table.shape[1], 1, page * n_heads)
            ks3 = ks.reshape(bs, n_heads, HEAD_DIM)
            vs3 = vs.reshape(bs, n_heads, HEAD_DIM)
            os3 = _decode_attn(cache_k, cache_v, j, page_table, bias,
                               qs.reshape(bs, n_heads, HEAD_DIM).astype(BF16), ks3, vs3)
            zs = os3.reshape(bs, d).astype(BF16)
            k_s.append(ks3.reshape(bs, 1, n_heads, HEAD_DIM))
            v_s.append(vs3.reshape(bs, 1, n_heads, HEAD_DIM))
            lf_s.append(lfs.reshape(bs, 1, n_heads))
        else:
            w = lru_conv_w.shape[-1]
            w_in = lru_w_in[j].astype(BF16)
            w_ai = jnp.concatenate([_block_diag(lru_w_a[j]), _block_diag(lru_w_i[j])], axis=1).astype(BF16)
            w_out = lru_w_out[j].astype(BF16)
            small = [row2(lru_b_in[j]), lru_conv_w[j], row2(lru_conv_b[j]), w_ai,
                     row2(lru_b_a[j]), row2(lru_b_i[j]), row2(lru_lambda[j])]
            zp, nhp, ncp = _lru_pre(xp, g_mix, w_in, *small)
            zs, nhs, xbs = _single_step_call(
                _lru_s_kernel,
                [xs, g_mix, w_in, *small, state_lru_h[j],
                 state_lru_conv[j, :, 0], state_lru_conv[j, :, 1], state_lru_conv[j, :, 2]],
                [jax.ShapeDtypeStruct((bs, w), BF16), jax.ShapeDtypeStruct((bs, w), F32),
                 jax.ShapeDtypeStruct((bs, w), F32)],
                "lru_sample")
            lh_p.append(nhp.reshape(bp, w))
            lh_s.append(nhs)
            lc_p.append(ncp)
            lc_s.append(jnp.concatenate([state_lru_conv[j, :, 1:], xbs[:, None, :]], axis=1))

        ffn_w = [row2(ffn_norm[i]), ffn_w_gate[i].astype(BF16), ffn_w_up[i].astype(BF16),
                 ffn_conv_w[i], row2(ffn_conv_b[i]), ffn_w_down[i].astype(BF16)]
        xp, nfp = _post(xp, zp, w_out, *ffn_w)
        xs, gs = _single_step_call(
            _post_s_kernel,
            [xs, zs, w_out, *ffn_w, state_ffn_conv[i, :, 0], state_ffn_conv[i, :, 1]],
            [jax.ShapeDtypeStruct((bs, d), F32), jax.ShapeDtypeStruct((bs, f), F32)],
            "post_ffn_sample")
        fc_p.append(nfp)
        fc_s.append(jnp.stack([state_ffn_conv[i, :, 1], gs], axis=1))

    return (xp, xs.reshape(bs, 1, d),
            jnp.stack(sc_p), jnp.stack(sc_s),
            jnp.stack(k_p), jnp.stack(v_p), jnp.stack(lf_p),
            jnp.stack(k_s), jnp.stack(v_s), jnp.stack(lf_s),
            jnp.stack(lh_p), jnp.stack(lh_s),
            jnp.stack(lc_p), jnp.stack(lc_s),
            jnp.stack(fc_p), jnp.stack(fc_s))
```

```python
import functools

import jax
import jax.numpy as jnp
from jax import lax
from jax.experimental import pallas as pl
from jax.experimental.pallas import tpu as pltpu

F32 = jnp.float32
BF16 = jnp.bfloat16

EPS = 1e-6
HEAD_DIM = 64
LRU_C = 8.0
MASK_VALUE = -1e30
LOG2E = 1.4426950408889634

SUBLANES = 8
LANES = 128
V7X_VMEM_LIMIT_BYTES = 56 * 1024 * 1024
ROW_TILE = 512
LRU_ROW_TILE = 256
DECODE_PAGES_PER_STEP = 8
LOGF_PAGES_PER_STEP = 8


def _dot(a, b):
    return jnp.dot(a, b, preferred_element_type=F32)


def _dot_nt(a, b):
    return lax.dot_general(a, b, (((1,), (1,)), ((), ())), preferred_element_type=F32)


def _split2(x):
    hi = x.astype(BF16)
    lo = (x - hi.astype(F32)).astype(BF16)
    return hi, lo


def _split3(x):
    hi = x.astype(BF16)
    r = x - hi.astype(F32)
    mid = r.astype(BF16)
    lo = (r - mid.astype(F32)).astype(BF16)
    return hi, mid, lo


def _dot_exact_rhs01(x_parts, m01):
    out = _dot(x_parts[0], m01)
    for p in x_parts[1:]:
        out = out + _dot(p, m01)
    return out


def _rmsnorm(x, g):
    ms = jnp.mean(x * x, axis=-1, keepdims=True)
    return x * lax.rsqrt(ms + EPS) * g


def _sigmoid(x):
    return 1.0 / (1.0 + jnp.exp(-x))


def _softplus(x):
    return jnp.maximum(x, 0.0) + jnp.log1p(jnp.exp(-jnp.abs(x)))


def _gelu_tanh(x):
    c = 0.7978845608028654
    return x * (0.5 * (1.0 + jnp.tanh(c * (x + 0.044715 * (x * x * x)))))


def _resident(shape):
    n = len(shape)
    return pl.BlockSpec(shape, lambda *_: (0,) * n, pipeline_mode=pl.Buffered(1))


def _params(*sem):
    return pltpu.CompilerParams(dimension_semantics=sem, vmem_limit_bytes=V7X_VMEM_LIMIT_BYTES)


def _row_tile(t, pref):
    return pref if t % pref == 0 else t


def _ffn_chunks(f):
    for n in (2, 1):
        if f % (n * LANES) == 0:
            return n
    return 1


def _head_rmsnorm(x, gain_tiled, seg, expand):
    ss = _dot_exact_rhs01(_split2(x * x), seg)
    inv = lax.rsqrt(ss * (1.0 / HEAD_DIM) + EPS)
    return x * _dot_exact_rhs01(_split3(inv), expand) * gain_tiled


def _log_sigmoid(x):
    return -_softplus(-x)


def _lru_gates(y, wai_ref, ba, bi, lam, w):
    yb = y.astype(BF16)
    r = _sigmoid(_dot(yb, wai_ref[:, 0:w]) + ba)
    i = _sigmoid(_dot(yb, wai_ref[:, w:2 * w]) + bi)
    log_a = (-LRU_C) * r * _softplus(-lam)
    a = jnp.exp(log_a)
    b = jnp.sqrt(1.0 - jnp.exp(2.0 * log_a)) * (i * y)
    return a, b


def _sconv_pre_kernel(x_ref, g_ref, win_ref, cw_ref, z_ref, nb_ref, ubuf):
    tm, d = x_ref.shape[1], x_ref.shape[2]

    @pl.when(pl.program_id(1) == 0)
    def _():
        ubuf[0:SUBLANES, :] = jnp.zeros((SUBLANES, d), F32)

    xn = _rmsnorm(x_ref[0], g_ref[...]).astype(BF16)
    bg = _dot(xn, win_ref[:, 0:d])
    u = _dot(xn, win_ref[:, d:2 * d]) * _dot(xn, win_ref[:, 2 * d:3 * d])
    ubuf[SUBLANES:SUBLANES + tm, :] = u
    y = (cw_ref[0:1, :] * ubuf[SUBLANES - 2:SUBLANES - 2 + tm, :]
         + cw_ref[1:2, :] * ubuf[SUBLANES - 1:SUBLANES - 1 + tm, :]
         + cw_ref[2:3, :] * u)
    z_ref[0] = (bg * y).astype(BF16)
    nb_ref[0] = ubuf[SUBLANES + tm - 2:SUBLANES + tm, :]
    ubuf[0:SUBLANES, :] = ubuf[tm:tm + SUBLANES, :]


def _sconv_pre(x, g, w_in, conv_w):
    b, t, d = x.shape
    tm = _row_tile(t, ROW_TILE)
    return pl.pallas_call(
        _sconv_pre_kernel,
        grid=(b, t // tm),
        in_specs=[
            pl.BlockSpec((1, tm, d), lambda i, j: (i, j, 0)),
            _resident((1, d)),
            _resident((d, 3 * d)),
            _resident((conv_w.shape[0], d)),
        ],
        out_specs=[
            pl.BlockSpec((1, tm, d), lambda i, j: (i, j, 0)),
            pl.BlockSpec((1, 2, d), lambda i, j: (i, 0, 0)),
        ],
        out_shape=[
            jax.ShapeDtypeStruct((b, t, d), BF16),
            jax.ShapeDtypeStruct((b, 2, d), F32),
        ],
        scratch_shapes=[pltpu.VMEM((SUBLANES + tm, d), F32)],
        compiler_params=_params("arbitrary", "arbitrary"),
        name="sconv_pre",
    )(x, g, w_in, conv_w)


def _ffn_body(x1, xn, gprev_fn, gstore_fn, wg_ref, wu_ref, cw_ref, cb_ref, wd_ref):
    f = wg_ref.shape[1]
    n_chunks = _ffn_chunks(f)
    fc = f // n_chunks
    acc = x1
    for c in range(n_chunks):
        sl = slice(c * fc, (c + 1) * fc)
        g = _dot(xn, wg_ref[:, sl])
        gstore_fn(sl, g)
        gc = (cw_ref[0:1, sl] * gprev_fn(sl, 2) + cw_ref[1:2, sl] * gprev_fn(sl, 1)
              + cw_ref[2:3, sl] * g + cb_ref[:, sl])
        up = _dot(xn, wu_ref[:, sl])
        hmid = (gc * _sigmoid(gc) * up).astype(BF16)
        acc = acc + _dot(hmid, wd_ref[sl, :])
    return acc


def _post_kernel(x_ref, z_ref, wo_ref, g_ref, wg_ref, wu_ref, cw_ref, cb_ref, wd_ref,
                 o_ref, nb_ref, gbuf):
    tm = x_ref.shape[1]
    f = wg_ref.shape[1]

    @pl.when(pl.program_id(1) == 0)
    def _():
        gbuf[0:SUBLANES, :] = jnp.zeros((SUBLANES, f), F32)

    x1 = x_ref[0] + _dot(z_ref[0], wo_ref[...])
    xn = _rmsnorm(x1, g_ref[...]).astype(BF16)

    def gstore(sl, g):
        gbuf[SUBLANES:SUBLANES + tm, sl] = g

    def gprev(sl, k):
        return gbuf[SUBLANES - k:SUBLANES - k + tm, sl]

    o_ref[0] = _ffn_body(x1, xn, gprev, gstore, wg_ref, wu_ref, cw_ref, cb_ref, wd_ref)
    nb_ref[0] = gbuf[SUBLANES + tm - 2:SUBLANES + tm, :]
    gbuf[0:SUBLANES, :] = gbuf[tm:tm + SUBLANES, :]


def _post(x, z, w_out, g, w_gate, w_up, conv_w, conv_b, w_down):
    b, t, d = x.shape
    kz = z.shape[2]
    f = w_gate.shape[1]
    tm = _row_tile(t, ROW_TILE)
    return pl.pallas_call(
        _post_kernel,
        grid=(b, t // tm),
        in_specs=[
            pl.BlockSpec((1, tm, d), lambda i, j: (i, j, 0)),
            pl.BlockSpec((1, tm, kz), lambda i, j: (i, j, 0)),
            _resident((kz, d)),
            _resident((1, d)),
            _resident((d, f)),
            _resident((d, f)),
            _resident((conv_w.shape[0], f)),
            _resident((1, f)),
            _resident((f, d)),
        ],
        out_specs=[
            pl.BlockSpec((1, tm, d), lambda i, j: (i, j, 0)),
            pl.BlockSpec((1, 2, f), lambda i, j: (i, 0, 0)),
        ],
        out_shape=[
            jax.ShapeDtypeStruct((b, t, d), F32),
            jax.ShapeDtypeStruct((b, 2, f), F32),
        ],
        scratch_shapes=[pltpu.VMEM((SUBLANES + tm, f), F32)],
        compiler_params=_params("arbitrary", "arbitrary"),
        name="post_ffn",
    )(x, z, w_out, g, w_gate, w_up, conv_w, conv_b, w_down)


def _fox_project(xn, wqkv_ref, wf_ref, bf_ref, qn_ref, kn_ref, seg_ref, exp_ref):
    d = xn.shape[1]
    q = _head_rmsnorm(_dot(xn, wqkv_ref[:, 0:d]), qn_ref[...], seg_ref[...], exp_ref[...])
    k = _head_rmsnorm(_dot(xn, wqkv_ref[:, d:2 * d]), kn_ref[...], seg_ref[...], exp_ref[...])
    v = _dot(xn, wqkv_ref[:, 2 * d:3 * d])
    lf = _log_sigmoid(_dot(xn, wf_ref[...]) + bf_ref[...])
    return q, k, v, lf


def _dot_exact_rhs01_lhs(m01, x_parts):
    out = _dot(m01, x_parts[0])
    for p in x_parts[1:]:
        out = out + _dot(m01, p)
    return out


def _fox_pre_kernel(x_ref, g_ref, wqkv_ref, wf_ref, bf_ref, qn_ref, kn_ref, seg_ref, exp_ref,
                    tri_ref, qa_ref, ka_ref, vt_ref, k_ref, v_ref, lf_ref, carry):
    tm = x_ref.shape[1]
    n_heads = lf_ref.shape[2]

    @pl.when(pl.program_id(1) == 0)
    def _():
        carry[...] = jnp.zeros_like(carry)

    xn = _rmsnorm(x_ref[0], g_ref[...]).astype(BF16)
    q, k, v, lf = _fox_project(xn, wqkv_ref, wf_ref, bf_ref, qn_ref, kn_ref, seg_ref, exp_ref)
    k_ref[0] = k
    v_ref[0] = v
    lf_ref[0] = lf[:, 0:n_heads]
    c = _dot_exact_rhs01_lhs(tri_ref[...], _split3(lf)) + carry[0:1, :]
    carry[0:1, :] = c[tm - 1:tm, :]

    lane = lax.broadcasted_iota(jnp.int32, (tm, LANES), 1)
    lane_row = lax.broadcasted_iota(jnp.int32, (1, LANES), 1)
    hi, mid, lo = (part.astype(F32) for part in _split3(c * LOG2E))
    bias_lo = jnp.where(lane < n_heads, hi,
                        jnp.where(lane < 2 * n_heads, pltpu.roll(mid, n_heads, 1),
                                  jnp.where(lane < 3 * n_heads, pltpu.roll(lo, 2 * n_heads, 1), 0.0)))
    bias_hi = pltpu.roll(bias_lo, HEAD_DIM, 1)
    lower = lane < HEAD_DIM
    qs = q * (HEAD_DIM ** -0.5 * LOG2E)
    for h in range(n_heads):
        col = slice((h // 2) * LANES, (h // 2 + 1) * LANES)
        base = h + (HEAD_DIM if h % 2 == 0 else 0)
        pick = jnp.where(lane_row == base, -1.0, 0.0)
        pick = pick + jnp.where(lane_row == base + n_heads, -1.0, 0.0)
        pick = pick + jnp.where(lane_row == base + 2 * n_heads, -1.0, 0.0)
        if h % 2 == 0:
            qa = jnp.where(lower, qs[:, col], pick)
            ka = jnp.where(lower, k[:, col], bias_hi)
        else:
            qa = jnp.where(lower, pick, qs[:, col])
            ka = jnp.where(lower, bias_lo, k[:, col])
        qa_ref[0, h] = qa.astype(BF16)
        ka_ref[0, h] = ka.astype(BF16)
    vt_ref[0, :, 0] = v.T.reshape(n_heads, HEAD_DIM, tm).astype(BF16)


def _fox_pre(x, g, w_qkv, w_f, b_f, qn, kn, seg, expand, tri, n_heads):
    b, t, d = x.shape
    tm = tri.shape[0]
    row = lambda i, j: (i, j, 0)
    head = lambda i, j: (i, 0, j, 0)
    return pl.pallas_call(
        _fox_pre_kernel,
        grid=(b, t // tm),
        in_specs=[
            pl.BlockSpec((1, tm, d), row),
            _resident((1, d)),
            _resident((d, 3 * d)),
            _resident((d, LANES)),
            _resident((1, LANES)),
            _resident((1, d)),
            _resident((1, d)),
            _resident((d, LANES)),
            _resident((LANES, d)),
            _resident((tm, tm)),
        ],
        out_specs=[
            pl.BlockSpec((1, n_heads, tm, LANES), head),
            pl.BlockSpec((1, n_heads, tm, LANES), head),
            pl.BlockSpec((1, n_heads, 1, HEAD_DIM, tm), lambda i, j: (i, 0, j, 0, 0)),
            pl.BlockSpec((1, tm, d), row),
            pl.BlockSpec((1, tm, d), row),
            pl.BlockSpec((1, tm, n_heads), row),
        ],
        out_shape=[
            jax.ShapeDtypeStruct((b, n_heads, t, LANES), BF16),
            jax.ShapeDtypeStruct((b, n_heads, t, LANES), BF16),
            jax.ShapeDtypeStruct((b, n_heads, t // tm, HEAD_DIM, tm), BF16),
            jax.ShapeDtypeStruct((b, t, d), F32),
            jax.ShapeDtypeStruct((b, t, d), F32),
            jax.ShapeDtypeStruct((b, t, n_heads), F32),
        ],
        scratch_shapes=[pltpu.VMEM((SUBLANES, LANES), F32)],
        compiler_params=_params("arbitrary", "arbitrary"),
        name="fox_pre",
    )(x, g, w_qkv, w_f, b_f, qn, kn, seg, expand, tri)


def _fox_attn_kernel(qa_ref, ka_ref, vt_ref, o_ref, acc_ref):
    pair, tq = qa_ref.shape[1], qa_ref.shape[2]
    qi = pl.program_id(2)
    key_idx = lax.broadcasted_iota(jnp.int32, (tq, tq), 0)
    qry_idx = lax.broadcasted_iota(jnp.int32, (tq, tq), 1)
    acc_ref[...] = jnp.zeros(acc_ref.shape, F32)

    def block(ki, stats, masked):
        start = pl.multiple_of(ki * tq, tq)
        out = []
        for j in range(pair):
            m_prev, l_prev = stats[j]
            st = _dot_nt(ka_ref[0, j, pl.ds(start, tq), :], qa_ref[0, j])
            if masked:
                st = jnp.where(key_idx <= qry_idx, st, MASK_VALUE)
            m_new = jnp.maximum(m_prev, jnp.max(st, axis=0, keepdims=True))
            alpha = jnp.exp2(m_prev - m_new)
            pt = jnp.exp2(st - m_new)
            l_new = alpha * l_prev + jnp.sum(pt, axis=0, keepdims=True)
            rows = slice(j * HEAD_DIM, (j + 1) * HEAD_DIM)
            acc_ref[rows, :] = alpha * acc_ref[rows, :] + _dot(vt_ref[0, j, ki], pt.astype(BF16))
            out.append((m_new, l_new))
        return tuple(out)

    init = tuple((jnp.full((1, tq), MASK_VALUE, F32), jnp.zeros((1, tq), F32)) for _ in range(pair))
    stats = lax.fori_loop(0, qi, lambda ki, st: block(ki, st, False), init)
    stats = block(qi, stats, True)
    for j in range(pair):
        rows = slice(j * HEAD_DIM, (j + 1) * HEAD_DIM)
        acc_ref[rows, :] = acc_ref[rows, :] / stats[j][1]
    o_ref[0] = acc_ref[...].T.astype(BF16)


def _fox_attn(qa, ka, vt):
    b, n_heads, t, _ = qa.shape
    nk, tq = vt.shape[2], vt.shape[4]
    pair = 2
    return pl.pallas_call(
        _fox_attn_kernel,
        grid=(b, n_heads // pair, t // tq),
        in_specs=[
            pl.BlockSpec((1, pair, tq, LANES), lambda i, h, j: (i, h, j, 0)),
            pl.BlockSpec((1, pair, t, LANES), lambda i, h, j: (i, h, 0, 0)),
            pl.BlockSpec((1, pair, nk, HEAD_DIM, tq), lambda i, h, j: (i, h, 0, 0, 0)),
        ],
        out_specs=pl.BlockSpec((1, tq, pair * HEAD_DIM), lambda i, h, j: (i, j, h)),
        out_shape=jax.ShapeDtypeStruct((b, t, n_heads * HEAD_DIM), BF16),
        scratch_shapes=[pltpu.VMEM((pair * HEAD_DIM, tq), F32)],
        compiler_params=_params("arbitrary", "arbitrary", "arbitrary"),
        name="fox_attn",
    )(qa, ka, vt)


def _shift_rows(x, s, fill, row_idx):
    return jnp.where(row_idx >= s, pltpu.roll(x, s, 0), fill)


def _lru_pre_kernel(x_ref, g_ref, win_ref, bin_ref, cw_ref, cb_ref, wai_ref, ba_ref, bi_ref,
                    lam_ref, z_ref, nh_ref, nc_ref, xbuf, hcarry):
    tm = x_ref.shape[1]
    w = cw_ref.shape[1]
    taps = cw_ref.shape[0]

    @pl.when(pl.program_id(1) == 0)
    def _():
        xbuf[0:SUBLANES, :] = jnp.zeros((SUBLANES, w), F32)
        hcarry[...] = jnp.zeros_like(hcarry)

    xn = _rmsnorm(x_ref[0], g_ref[...]).astype(BF16)
    gate = _gelu_tanh(_dot(xn, win_ref[:, 0:w]) + bin_ref[:, 0:w])
    xb = _dot(xn, win_ref[:, w:2 * w]) + bin_ref[:, w:2 * w]
    xbuf[SUBLANES:SUBLANES + tm, :] = xb
    y = cw_ref[taps - 1:taps, :] * xb + cb_ref[...]
    for k in range(1, taps):
        y = y + cw_ref[taps - 1 - k:taps - k, :] * xbuf[SUBLANES - k:SUBLANES - k + tm, :]
    nc_ref[0] = xbuf[SUBLANES + tm - (taps - 1):SUBLANES + tm, :]
    xbuf[0:SUBLANES, :] = xbuf[tm:tm + SUBLANES, :]

    a, bt = _lru_gates(y, wai_ref, ba_ref[...], bi_ref[...], lam_ref[...], w)
    row_idx = lax.broadcasted_iota(jnp.int32, (tm, w), 0)
    s = 1
    while s < tm:
        bt = a * _shift_rows(bt, s, 0.0, row_idx) + bt
        a = a * _shift_rows(a, s, 1.0, row_idx)
        s *= 2
    h = a * hcarry[0:1, :] + bt
    hcarry[0:1, :] = h[tm - 1:tm, :]
    nh_ref[0] = h[tm - 1:tm, :]
    z_ref[0] = (h * gate).astype(BF16)


def _lru_pre(x, g, w_in, b_in, conv_w, conv_b, w_ai, b_a, b_i, lam):
    b, t, d = x.shape
    w = conv_w.shape[1]
    taps = conv_w.shape[0]
    tm = _row_tile(t, LRU_ROW_TILE)
    return pl.pallas_call(
        _lru_pre_kernel,
        grid=(b, t // tm),
        in_specs=[
            pl.BlockSpec((1, tm, d), lambda i, j: (i, j, 0)),
            _resident((1, d)),
            _resident((d, 2 * w)),
            _resident((1, 2 * w)),
            _resident((taps, w)),
            _resident((1, w)),
            _resident((w, 2 * w)),
            _resident((1, w)),
            _resident((1, w)),
            _resident((1, w)),
        ],
        out_specs=[
            pl.BlockSpec((1, tm, w), lambda i, j: (i, j, 0)),
            pl.BlockSpec((1, 1, w), lambda i, j: (i, 0, 0)),
            pl.BlockSpec((1, taps - 1, w), lambda i, j: (i, 0, 0)),
        ],
        out_shape=[
            jax.ShapeDtypeStruct((b, t, w), BF16),
            jax.ShapeDtypeStruct((b, 1, w), F32),
            jax.ShapeDtypeStruct((b, taps - 1, w), F32),
        ],
        scratch_shapes=[
            pltpu.VMEM((SUBLANES + tm, w), F32),
            pltpu.VMEM((SUBLANES, w), F32),
        ],
        compiler_params=_params("arbitrary", "arbitrary"),
        name="lru_pre",
    )(x, g, w_in, b_in, conv_w, conv_b, w_ai, b_a, b_i, lam)


def _whole(shape):
    n = len(shape)
    return pl.BlockSpec(shape, lambda *_: (0,) * n)


def _single_step_call(kernel, inputs, out_shapes, name):
    return pl.pallas_call(
        kernel,
        grid=(1,),
        in_specs=[_resident(a.shape) for a in inputs],
        out_specs=[_whole(s.shape) for s in out_shapes],
        out_shape=out_shapes,
        compiler_params=_params("arbitrary"),
        name=name,
    )(*inputs)


def _sconv_s_kernel(x_ref, g_ref, win_ref, cw_ref, b0_ref, b1_ref, z_ref, u_ref):
    d = x_ref.shape[1]
    xn = _rmsnorm(x_ref[...], g_ref[...]).astype(BF16)
    bg = _dot(xn, win_ref[:, 0:d])
    u = _dot(xn, win_ref[:, d:2 * d]) * _dot(xn, win_ref[:, 2 * d:3 * d])
    y = cw_ref[0:1, :] * b0_ref[...] + cw_ref[1:2, :] * b1_ref[...] + cw_ref[2:3, :] * u
    z_ref[...] = (bg * y).astype(BF16)
    u_ref[...] = u


def _post_s_kernel(x_ref, z_ref, wo_ref, g_ref, wg_ref, wu_ref, cw_ref, cb_ref, wd_ref,
                   b0_ref, b1_ref, o_ref, gn_ref):
    x1 = x_ref[...] + _dot(z_ref[...], wo_ref[...])
    xn = _rmsnorm(x1, g_ref[...]).astype(BF16)

    def gstore(sl, g):
        gn_ref[:, sl] = g

    def gprev(sl, k):
        return (b1_ref if k == 1 else b0_ref)[:, sl]

    o_ref[...] = _ffn_body(x1, xn, gprev, gstore, wg_ref, wu_ref, cw_ref, cb_ref, wd_ref)


def _fox_pre_s_kernel(x_ref, g_ref, wqkv_ref, wf_ref, bf_ref, qn_ref, kn_ref, seg_ref, exp_ref,
                      q_ref, k_ref, v_ref, lf_ref):
    xn = _rmsnorm(x_ref[...], g_ref[...]).astype(BF16)
    q, k, v, lf = _fox_project(xn, wqkv_ref, wf_ref, bf_ref, qn_ref, kn_ref, seg_ref, exp_ref)
    q_ref[...] = q * (HEAD_DIM ** -0.5)
    k_ref[...] = k
    v_ref[...] = v
    lf_ref[...] = lf


def _lru_s_kernel(x_ref, g_ref, win_ref, bin_ref, cw_ref, cb_ref, wai_ref, ba_ref, bi_ref, lam_ref,
                  h0_ref, c0_ref, c1_ref, c2_ref, z_ref, h_ref, xb_ref):
    w = cw_ref.shape[1]
    xn = _rmsnorm(x_ref[...], g_ref[...]).astype(BF16)
    gate = _gelu_tanh(_dot(xn, win_ref[:, 0:w]) + bin_ref[:, 0:w])
    xb = _dot(xn, win_ref[:, w:2 * w]) + bin_ref[:, w:2 * w]
    y = (cw_ref[0:1, :] * c0_ref[...] + cw_ref[1:2, :] * c1_ref[...] + cw_ref[2:3, :] * c2_ref[...]
         + cw_ref[3:4, :] * xb + cb_ref[...])
    a, bt = _lru_gates(y, wai_ref, ba_ref[...], bi_ref[...], lam_ref[...], w)
    h = a * h0_ref[...] + bt
    h_ref[...] = h
    xb_ref[...] = xb
    z_ref[...] = (h * gate).astype(BF16)


def _logf_suffix_kernel(pt_ref, *refs, n_slots):
    lf_refs = refs[:n_slots]
    lfn_ref, later_ref, out_ref, carry = refs[n_slots:]

    @pl.when(pl.program_id(1) == 0)
    def _():
        carry[...] = jnp.zeros_like(carry)

    for i in reversed(range(n_slots)):
        lf = lf_refs[i][...]
        within = _dot_exact_rhs01(_split3(lf), later_ref[...])
        out_ref[0, i] = within + carry[...] + lfn_ref[0]
        carry[...] = carry[...] + jnp.sum(lf, axis=1, keepdims=True)


def _logf_suffix(logf_t, layer, page_table, lf_new, later):
    bs, n_pages = page_table.shape
    n_heads, page = logf_t.shape[2], logf_t.shape[3]
    g = LOGF_PAGES_PER_STEP if n_pages % LOGF_PAGES_PER_STEP == 0 else 1
    n_groups = n_pages // g

    def page_spec(i):
        def idx(b, p, pt):
            return (layer, pt[b, (n_groups - 1 - p) * g + i], 0, 0)
        return pl.BlockSpec((None, None, n_heads, page), idx)

    grid_spec = pltpu.PrefetchScalarGridSpec(
        num_scalar_prefetch=1,
        grid=(bs, n_groups),
        in_specs=[page_spec(i) for i in range(g)] + [
            pl.BlockSpec((1, n_heads, 1), lambda b, p, pt: (b, 0, 0)),
            pl.BlockSpec((page, page), lambda b, p, pt: (0, 0)),
        ],
        out_specs=pl.BlockSpec((1, g, n_heads, page), lambda b, p, pt: (b, n_groups - 1 - p, 0, 0)),
        scratch_shapes=[pltpu.VMEM((n_heads, 1), F32)],
    )
    return pl.pallas_call(
        functools.partial(_logf_suffix_kernel, n_slots=g),
        grid_spec=grid_spec,
        out_shape=jax.ShapeDtypeStruct((bs, n_pages, n_heads, page), F32),
        compiler_params=_params("arbitrary", "arbitrary"),
        name="logf_suffix",
    )(page_table, *([logf_t] * g), lf_new, later)


def _decode_attn_kernel(pt_ref, *refs, n_slots):
    k_refs = refs[:n_slots]
    v_refs = refs[n_slots:2 * n_slots]
    b_refs = refs[2 * n_slots:3 * n_slots]
    q_ref, kn_ref, qb_ref, vnb_ref, o_ref, m_ref, l_ref, acc_ref = refs[3 * n_slots:]
    n_heads = q_ref.shape[1]

    @pl.when(pl.program_id(1) == 0)
    def _():
        m_ref[...] = jnp.sum(q_ref[0] * kn_ref[0], axis=1, keepdims=True)
        l_ref[...] = jnp.ones_like(l_ref)
        lane = lax.broadcasted_iota(jnp.int32, acc_ref.shape[1:], 1)
        for h in range(n_heads):
            acc_ref[h] = jnp.where(lane == 0, vnb_ref[0, h], 0.0)

    logits = []
    for i in range(n_slots):
        rows = [jnp.sum(k_refs[i][h] * qb_ref[0, h], axis=0, keepdims=True) for h in range(n_heads)]
        logits.append(jnp.concatenate(rows, axis=0) + b_refs[i][0, 0])
    m_prev = m_ref[...]
    m_new = m_prev
    for s in logits:
        m_new = jnp.maximum(m_new, jnp.max(s, axis=1, keepdims=True))
    alpha = jnp.exp(m_prev - m_new)
    probs = [jnp.exp(s - m_new) for s in logits]
    l_new = alpha * l_ref[...]
    for p in probs:
        l_new = l_new + jnp.sum(p, axis=1, keepdims=True)
    l_ref[...] = l_new
    m_ref[...] = m_new
    for h in range(n_heads):
        a = acc_ref[h] * alpha[h:h + 1, :]
        for i in range(n_slots):
            a = a + probs[i][h:h + 1, :] * v_refs[i][h]
        acc_ref[h] = a

    @pl.when(pl.program_id(1) == pl.num_programs(1) - 1)
    def _():
        for h in range(n_heads):
            o_ref[0, h] = jnp.sum(acc_ref[h], axis=1, keepdims=True) / l_ref[h:h + 1, :]


def _decode_attn(k_t, v_t, layer, page_table, bias, q3, kn3, qb, vnb):
    bs, n_pages = page_table.shape
    n_heads, page = k_t.shape[2], k_t.shape[4]
    g = DECODE_PAGES_PER_STEP if n_pages % DECODE_PAGES_PER_STEP == 0 else 1
    n_groups = n_pages // g

    def page_spec(i):
        def idx(b, p, pt):
            return (layer, pt[b, p * g + i], 0, 0, 0)
        return pl.BlockSpec((None, None, n_heads, HEAD_DIM, page), idx)

    def bias_spec(i):
        return pl.BlockSpec((1, 1, n_heads, page), lambda b, p, pt: (b, p * g + i, 0, 0))

    per_seq = pl.BlockSpec((1, n_heads, HEAD_DIM), lambda b, p, pt: (b, 0, 0))
    per_seq_b = pl.BlockSpec((1, n_heads, HEAD_DIM, page), lambda b, p, pt: (b, 0, 0, 0))
    grid_spec = pltpu.PrefetchScalarGridSpec(
        num_scalar_prefetch=1,
        grid=(bs, n_groups),
        in_specs=([page_spec(i) for i in range(g)] + [page_spec(i) for i in range(g)]
                  + [bias_spec(i) for i in range(g)] + [per_seq, per_seq, per_seq_b, per_seq_b]),
        out_specs=pl.BlockSpec((1, n_heads, HEAD_DIM, 1), lambda b, p, pt: (b, 0, 0, 0)),
        scratch_shapes=[
            pltpu.VMEM((n_heads, 1), F32),
            pltpu.VMEM((n_heads, 1), F32),
            pltpu.VMEM((n_heads, HEAD_DIM, page), F32),
        ],
    )
    return pl.pallas_call(
        functools.partial(_decode_attn_kernel, n_slots=g),
        grid_spec=grid_spec,
        out_shape=jax.ShapeDtypeStruct((bs, n_heads, HEAD_DIM, 1), F32),
        compiler_params=_params("arbitrary", "arbitrary"),
        name="decode_attn",
    )(page_table, *([k_t] * g), *([v_t] * g), *([bias] * g), q3, kn3, qb, vnb)


def _block_diag(w):
    n, c, _ = w.shape
    eye = jnp.eye(n, dtype=w.dtype)
    return (eye[:, None, :, None] * w[:, :, None, :]).reshape(n * c, n * c)


def kernel(x_prompt, x_sample, state_sconv, cache_k, cache_v, cache_logf, page_table, state_lru_h, state_lru_conv, state_ffn_conv, mix_norm, ffn_norm, sc_w_in, sc_conv_w, sc_w_out, fox_w_qkv, fox_w_f, fox_b_f, fox_q_norm, fox_k_norm, fox_w_o, lru_w_in, lru_b_in, lru_conv_w, lru_conv_b, lru_w_a, lru_b_a, lru_w_i, lru_b_i, lru_lambda, lru_w_out, ffn_w_gate, ffn_w_up, ffn_conv_w, ffn_conv_b, ffn_w_down):
    depth, d = mix_norm.shape
    bp, tp, _ = x_prompt.shape
    bs, ts, _ = x_sample.shape
    assert ts == 1, "the sample group carries one new row per sequence"
    n_heads = fox_w_f.shape[-1]
    assert n_heads * HEAD_DIM == d and n_heads % 2 == 0 and 3 * n_heads <= HEAD_DIM
    n_mixers = 3
    f = ffn_w_gate.shape[-1]
    page = cache_k.shape[2]

    xp = x_prompt
    xs = x_sample.reshape(bs, d)
    row2 = lambda v: v.reshape(1, -1)

    lane_head = jnp.arange(d, dtype=jnp.int32) // HEAD_DIM
    seg = (lane_head[:, None] == jnp.arange(LANES, dtype=jnp.int32)[None, :]).astype(BF16)
    expand = seg.T
    tm_fox = _row_tile(tp, ROW_TILE)
    tri = (jnp.arange(tm_fox)[:, None] >= jnp.arange(tm_fox)[None, :]).astype(BF16)
    later = (jnp.arange(page)[:, None] > jnp.arange(page)[None, :]).astype(BF16)
    k_t = jnp.transpose(cache_k, (0, 1, 3, 4, 2))
    v_t = jnp.transpose(cache_v, (0, 1, 3, 4, 2))
    logf_t = jnp.transpose(cache_logf, (0, 1, 3, 2))

    sc_p, sc_s, fc_p, fc_s = [], [], [], []
    k_p, v_p, lf_p, k_s, v_s, lf_s = [], [], [], [], [], []
    lh_p, lh_s, lc_p, lc_s = [], [], [], []

    for i in range(depth):
        j = i // n_mixers
        g_mix = row2(mix_norm[i])
        if i % n_mixers == 0:
            w_in = sc_w_in[j].astype(BF16)
            w_out = sc_w_out[j].astype(BF16)
            zp, nbp = _sconv_pre(xp, g_mix, w_in, sc_conv_w[j])
            zs, us = _single_step_call(
                _sconv_s_kernel,
                [xs, g_mix, w_in, sc_conv_w[j], state_sconv[j, :, 0], state_sconv[j, :, 1]],
                [jax.ShapeDtypeStruct((bs, d), BF16), jax.ShapeDtypeStruct((bs, d), F32)],
                "sconv_sample")
            sc_p.append(nbp)
            sc_s.append(jnp.stack([state_sconv[j, :, 1], us], axis=1))
        elif i % n_mixers == 1:
            w_qkv = fox_w_qkv[j].astype(BF16)
            w_f = jnp.pad(fox_w_f[j], ((0, 0), (0, LANES - n_heads))).astype(BF16)
            b_f = jnp.pad(fox_b_f[j], (0, LANES - n_heads)).reshape(1, LANES)
            qn = row2(jnp.tile(fox_q_norm[j], n_heads))
            kn = row2(jnp.tile(fox_k_norm[j], n_heads))
            w_out = fox_w_o[j].astype(BF16)
            qa, ka, vt, kp, vp, lfp = _fox_pre(xp, g_mix, w_qkv, w_f, b_f, qn, kn, seg, expand, tri, n_heads)
            zp = _fox_attn(qa, ka, vt)
            k_p.append(kp.reshape(bp, tp, n_heads, HEAD_DIM))
            v_p.append(vp.reshape(bp, tp, n_heads, HEAD_DIM))
            lf_p.append(lfp)

            qs, ks, vs, lfs = _single_step_call(
                _fox_pre_s_kernel,
                [xs, g_mix, w_qkv, w_f, b_f, qn, kn, seg, expand],
                [jax.ShapeDtypeStruct((bs, d), F32)] * 3 + [jax.ShapeDtypeStruct((bs, LANES), F32)],
                "fox_pre_sample")
            lfs = lfs[:, :n_heads]
            bias = _logf_suffix(logf_t, j, page_table, lfs.reshape(bs, n_heads, 1), later)
            qs3 = qs.reshape(bs, n_heads, HEAD_DIM)
            ks3 = ks.reshape(bs, n_heads, HEAD_DIM)
            vs3 = vs.reshape(bs, n_heads, HEAD_DIM)
            along_lanes = lambda a: jnp.broadcast_to(a[..., None], (bs, n_heads, HEAD_DIM, page))
            os4 = _decode_attn(k_t, v_t, j, page_table, bias, qs3, ks3, along_lanes(qs3), along_lanes(vs3))
            zs = os4.reshape(bs, d).astype(BF16)
            k_s.append(ks3.reshape(bs, 1, n_heads, HEAD_DIM))
            v_s.append(vs3.reshape(bs, 1, n_heads, HEAD_DIM))
            lf_s.append(lfs.reshape(bs, 1, n_heads))
        else:
            w = lru_conv_w.shape[-1]
            w_in = lru_w_in[j].astype(BF16)
            w_ai = jnp.concatenate([_block_diag(lru_w_a[j]), _block_diag(lru_w_i[j])], axis=1).astype(BF16)
            w_out = lru_w_out[j].astype(BF16)
            small = [row2(lru_b_in[j]), lru_conv_w[j], row2(lru_conv_b[j]), w_ai,
                     row2(lru_b_a[j]), row2(lru_b_i[j]), row2(lru_lambda[j])]
            zp, nhp, ncp = _lru_pre(xp, g_mix, w_in, *small)
            zs, nhs, xbs = _single_step_call(
                _lru_s_kernel,
                [xs, g_mix, w_in, *small, state_lru_h[j],
                 state_lru_conv[j, :, 0], state_lru_conv[j, :, 1], state_lru_conv[j, :, 2]],
                [jax.ShapeDtypeStruct((bs, w), BF16), jax.ShapeDtypeStruct((bs, w), F32),
                 jax.ShapeDtypeStruct((bs, w), F32)],
                "lru_sample")
            lh_p.append(nhp.reshape(bp, w))
            lh_s.append(nhs)
            lc_p.append(ncp)
            lc_s.append(jnp.concatenate([state_lru_conv[j, :, 1:], xbs[:, None, :]], axis=1))

        ffn_w = [row2(ffn_norm[i]), ffn_w_gate[i].astype(BF16), ffn_w_up[i].astype(BF16),
                 ffn_conv_w[i], row2(ffn_conv_b[i]), ffn_w_down[i].astype(BF16)]
        xp, nfp = _post(xp, zp, w_out, *ffn_w)
        xs, gs = _single_step_call(
            _post_s_kernel,
            [xs, zs, w_out, *ffn_w, state_ffn_conv[i, :, 0], state_ffn_conv[i, :, 1]],
            [jax.ShapeDtypeStruct((bs, d), F32), jax.ShapeDtypeStruct((bs, f), F32)],
            "post_ffn_sample")
        fc_p.append(nfp)
        fc_s.append(jnp.stack([state_ffn_conv[i, :, 1], gs], axis=1))

    return (xp, xs.reshape(bs, 1, d),
            jnp.stack(sc_p), jnp.stack(sc_s),
            jnp.stack(k_p), jnp.stack(v_p), jnp.stack(lf_p),
            jnp.stack(k_s), jnp.stack(v_s), jnp.stack(lf_s),
            jnp.stack(lh_p), jnp.stack(lh_s),
            jnp.stack(lc_p), jnp.stack(lc_s),
            jnp.stack(fc_p), jnp.stack(fc_s))
```

```python
import functools

import jax
import jax.numpy as jnp
from jax import lax
from jax.experimental import pallas as pl
from jax.experimental.pallas import tpu as pltpu

F32 = jnp.float32
BF16 = jnp.bfloat16

EPS = 1e-6
HEAD_DIM = 64
LRU_C = 8.0
MASK_VALUE = -1e30
LOG2E = 1.4426950408889634
VT_ROWS = HEAD_DIM + 16

SUBLANES = 8
LANES = 128
V7X_VMEM_LIMIT_BYTES = 56 * 1024 * 1024
ROW_TILE = 512
LRU_ROW_TILE = 256
DECODE_PAGES_PER_STEP = 8


def _dot(a, b):
    return jnp.dot(a, b, preferred_element_type=F32)


def _dot_nt(a, b):
    return lax.dot_general(a, b, (((1,), (1,)), ((), ())), preferred_element_type=F32)


def _split2(x):
    hi = x.astype(BF16)
    lo = (x - hi.astype(F32)).astype(BF16)
    return hi, lo


def _split3(x):
    hi = x.astype(BF16)
    r = x - hi.astype(F32)
    mid = r.astype(BF16)
    lo = (r - mid.astype(F32)).astype(BF16)
    return hi, mid, lo


def _dot_exact_rhs01(x_parts, m01):
    out = _dot(x_parts[0], m01)
    for p in x_parts[1:]:
        out = out + _dot(p, m01)
    return out


def _rmsnorm(x, g):
    ms = jnp.mean(x * x, axis=-1, keepdims=True)
    return x * lax.rsqrt(ms + EPS) * g


def _sigmoid(x):
    return 1.0 / (1.0 + jnp.exp(-x))


def _softplus(x):
    return jnp.maximum(x, 0.0) + jnp.log1p(jnp.exp(-jnp.abs(x)))


def _gelu_tanh(x):
    c2 = 2.0 * 0.7978845608028654
    return x * _sigmoid(c2 * (x + 0.044715 * (x * x * x)))


def _resident(shape):
    n = len(shape)
    return pl.BlockSpec(shape, lambda *_: (0,) * n, pipeline_mode=pl.Buffered(1))


def _params(*sem):
    return pltpu.CompilerParams(dimension_semantics=sem, vmem_limit_bytes=V7X_VMEM_LIMIT_BYTES)


def _row_tile(t, pref):
    return pref if t % pref == 0 else t


def _ffn_chunks(f):
    for n in (2, 1):
        if f % (n * LANES) == 0:
            return n
    return 1


def _head_rmsnorm(x, gain_tiled, seg, expand):
    ss = _dot_exact_rhs01(_split2(x * x), seg)
    inv = lax.rsqrt(ss * (1.0 / HEAD_DIM) + EPS)
    return x * _dot_exact_rhs01(_split2(inv), expand) * gain_tiled


def _log_sigmoid(x):
    return -_softplus(-x)


def _lru_gates(y, wai_ref, ba, bi, lam, w):
    yb = y.astype(BF16)
    half = w // 2
    parts = [_dot(yb[:, k * half:(k + 1) * half], wai_ref[k]) for k in range(2)]
    r = _sigmoid(jnp.concatenate([pt[:, 0:half] for pt in parts], axis=1) + ba)
    i = _sigmoid(jnp.concatenate([pt[:, half:2 * half] for pt in parts], axis=1) + bi)
    log_a = (-LRU_C) * r * _softplus(-lam)
    a = jnp.exp(log_a)
    b = jnp.sqrt(1.0 - a * a) * (i * y)
    return a, b


def _sconv_pre_kernel(x_ref, g_ref, win_ref, cw_ref, z_ref, nb_ref, ubuf):
    tm, d = x_ref.shape[1], x_ref.shape[2]

    @pl.when(pl.program_id(1) == 0)
    def _():
        ubuf[0:SUBLANES, :] = jnp.zeros((SUBLANES, d), F32)

    xn = _rmsnorm(x_ref[0], g_ref[...]).astype(BF16)
    bg = _dot(xn, win_ref[:, 0:d])
    u = _dot(xn, win_ref[:, d:2 * d]) * _dot(xn, win_ref[:, 2 * d:3 * d])
    ubuf[SUBLANES:SUBLANES + tm, :] = u
    y = (cw_ref[0:1, :] * ubuf[SUBLANES - 2:SUBLANES - 2 + tm, :]
         + cw_ref[1:2, :] * ubuf[SUBLANES - 1:SUBLANES - 1 + tm, :]
         + cw_ref[2:3, :] * u)
    z_ref[0] = (bg * y).astype(BF16)
    nb_ref[0] = ubuf[SUBLANES + tm - 2:SUBLANES + tm, :]
    ubuf[0:SUBLANES, :] = ubuf[tm:tm + SUBLANES, :]


def _sconv_pre(x, g, w_in, conv_w):
    b, t, d = x.shape
    tm = _row_tile(t, ROW_TILE)
    return pl.pallas_call(
        _sconv_pre_kernel,
        grid=(b, t // tm),
        in_specs=[
            pl.BlockSpec((1, tm, d), lambda i, j: (i, j, 0)),
            _resident((1, d)),
            _resident((d, 3 * d)),
            _resident((conv_w.shape[0], d)),
        ],
        out_specs=[
            pl.BlockSpec((1, tm, d), lambda i, j: (i, j, 0)),
            pl.BlockSpec((1, 2, d), lambda i, j: (i, 0, 0)),
        ],
        out_shape=[
            jax.ShapeDtypeStruct((b, t, d), BF16),
            jax.ShapeDtypeStruct((b, 2, d), F32),
        ],
        scratch_shapes=[pltpu.VMEM((SUBLANES + tm, d), F32)],
        compiler_params=_params("arbitrary", "arbitrary"),
        name="sconv_pre",
    )(x, g, w_in, conv_w)


def _ffn_body(x1, xn, gprev_fn, gstore_fn, wg_ref, wu_ref, cw_ref, cb_ref, wd_ref):
    f = wg_ref.shape[1]
    n_chunks = _ffn_chunks(f)
    fc = f // n_chunks
    acc = x1
    for c in range(n_chunks):
        sl = slice(c * fc, (c + 1) * fc)
        g = _dot(xn, wg_ref[:, sl])
        gstore_fn(sl, g)
        gc = (cw_ref[0:1, sl] * gprev_fn(sl, 2) + cw_ref[1:2, sl] * gprev_fn(sl, 1)
              + cw_ref[2:3, sl] * g + cb_ref[:, sl])
        up = _dot(xn, wu_ref[:, sl])
        hmid = (gc * _sigmoid(gc) * up).astype(BF16)
        acc = acc + _dot(hmid, wd_ref[sl, :])
    return acc


def _post_kernel(x_ref, z_ref, wo_ref, g_ref, wg_ref, wu_ref, cw_ref, cb_ref, wd_ref,
                 o_ref, nb_ref, gbuf):
    tm = x_ref.shape[1]
    f = wg_ref.shape[1]

    @pl.when(pl.program_id(1) == 0)
    def _():
        gbuf[0:SUBLANES, :] = jnp.zeros((SUBLANES, f), F32)

    x1 = x_ref[0] + _dot(z_ref[0], wo_ref[...])
    xn = _rmsnorm(x1, g_ref[...]).astype(BF16)

    def gstore(sl, g):
        gbuf[SUBLANES:SUBLANES + tm, sl] = g

    def gprev(sl, k):
        return gbuf[SUBLANES - k:SUBLANES - k + tm, sl]

    o_ref[0] = _ffn_body(x1, xn, gprev, gstore, wg_ref, wu_ref, cw_ref, cb_ref, wd_ref)
    nb_ref[0] = gbuf[SUBLANES + tm - 2:SUBLANES + tm, :]
    gbuf[0:SUBLANES, :] = gbuf[tm:tm + SUBLANES, :]


def _post(x, z, w_out, g, w_gate, w_up, conv_w, conv_b, w_down):
    b, t, d = x.shape
    kz = z.shape[2]
    f = w_gate.shape[1]
    tm = _row_tile(t, ROW_TILE)
    return pl.pallas_call(
        _post_kernel,
        grid=(b, t // tm),
        in_specs=[
            pl.BlockSpec((1, tm, d), lambda i, j: (i, j, 0)),
            pl.BlockSpec((1, tm, kz), lambda i, j: (i, j, 0)),
            _resident((kz, d)),
            _resident((1, d)),
            _resident((d, f)),
            _resident((d, f)),
            _resident((conv_w.shape[0], f)),
            _resident((1, f)),
            _resident((f, d)),
        ],
        out_specs=[
            pl.BlockSpec((1, tm, d), lambda i, j: (i, j, 0)),
            pl.BlockSpec((1, 2, f), lambda i, j: (i, 0, 0)),
        ],
        out_shape=[
            jax.ShapeDtypeStruct((b, t, d), F32),
            jax.ShapeDtypeStruct((b, 2, f), F32),
        ],
        scratch_shapes=[pltpu.VMEM((SUBLANES + tm, f), F32)],
        compiler_params=_params("arbitrary", "arbitrary"),
        name="post_ffn",
    )(x, z, w_out, g, w_gate, w_up, conv_w, conv_b, w_down)


def _fox_project(xn, wqkv_ref, wf_ref, bf_ref, qn_ref, kn_ref, seg_ref, exp_ref):
    d = xn.shape[1]
    q = _head_rmsnorm(_dot(xn, wqkv_ref[:, 0:d]), qn_ref[...], seg_ref[...], exp_ref[...])
    k = _head_rmsnorm(_dot(xn, wqkv_ref[:, d:2 * d]), kn_ref[...], seg_ref[...], exp_ref[...])
    v = _dot(xn, wqkv_ref[:, 2 * d:3 * d])
    lf = _log_sigmoid(_dot(xn, wf_ref[...]) + bf_ref[...])
    return q, k, v, lf


def _dot_exact_rhs01_lhs(m01, x_parts):
    out = _dot(m01, x_parts[0])
    for p in x_parts[1:]:
        out = out + _dot(m01, p)
    return out


def _fox_pre_kernel(x_ref, g_ref, wqkv_ref, wf_ref, bf_ref, qn_ref, kn_ref, seg_ref, exp_ref,
                    tri_ref, qa_ref, ka_ref, vt_ref, k_ref, v_ref, lf_ref, carry):
    tm = x_ref.shape[1]
    n_heads = lf_ref.shape[2]

    @pl.when(pl.program_id(1) == 0)
    def _():
        carry[...] = jnp.zeros_like(carry)

    xn = _rmsnorm(x_ref[0], g_ref[...]).astype(BF16)
    q, k, v, lf = _fox_project(xn, wqkv_ref, wf_ref, bf_ref, qn_ref, kn_ref, seg_ref, exp_ref)
    k_ref[0] = k
    v_ref[0] = v
    lf_ref[0] = lf[:, 0:n_heads]
    c = _dot_exact_rhs01_lhs(tri_ref[...], _split3(lf)) + carry[0:1, :]
    carry[0:1, :] = c[tm - 1:tm, :]

    lane = lax.broadcasted_iota(jnp.int32, (tm, LANES), 1)
    lane_row = lax.broadcasted_iota(jnp.int32, (1, LANES), 1)
    hi, mid, lo = (part.astype(F32) for part in _split3(c * LOG2E))
    bias_lo = jnp.where(lane < n_heads, hi,
                        jnp.where(lane < 2 * n_heads, pltpu.roll(mid, n_heads, 1),
                                  jnp.where(lane < 3 * n_heads, pltpu.roll(lo, 2 * n_heads, 1), 0.0)))
    bias_hi = pltpu.roll(bias_lo, HEAD_DIM, 1)
    lower = lane < HEAD_DIM
    qs = q * (HEAD_DIM ** -0.5 * LOG2E)
    for h in range(n_heads):
        col = slice((h // 2) * LANES, (h // 2 + 1) * LANES)
        base = h + (HEAD_DIM if h % 2 == 0 else 0)
        pick = jnp.where(lane_row == base, -1.0, 0.0)
        pick = pick + jnp.where(lane_row == base + n_heads, -1.0, 0.0)
        pick = pick + jnp.where(lane_row == base + 2 * n_heads, -1.0, 0.0)
        if h % 2 == 0:
            qa = jnp.where(lower, qs[:, col], pick)
            ka = jnp.where(lower, k[:, col], bias_hi)
        else:
            qa = jnp.where(lower, pick, qs[:, col])
            ka = jnp.where(lower, bias_lo, k[:, col])
        qa_ref[0, h] = qa.astype(BF16)
        ka_ref[0, h] = ka.astype(BF16)
    vt_ref[0, :, 0, 0:HEAD_DIM, :] = v.T.reshape(n_heads, HEAD_DIM, tm).astype(BF16)
    pad_rows = vt_ref.shape[3] - HEAD_DIM
    first = lax.broadcasted_iota(jnp.int32, (n_heads, pad_rows, tm), 1) == 0
    vt_ref[0, :, 0, HEAD_DIM:HEAD_DIM + pad_rows, :] = jnp.where(first, 1.0, 0.0).astype(BF16)


def _fox_pre(x, g, w_qkv, w_f, b_f, qn, kn, seg, expand, tri, n_heads):
    b, t, d = x.shape
    tm = tri.shape[0]
    row = lambda i, j: (i, j, 0)
    head = lambda i, j: (i, 0, j, 0)
    return pl.pallas_call(
        _fox_pre_kernel,
        grid=(b, t // tm),
        in_specs=[
            pl.BlockSpec((1, tm, d), row),
            _resident((1, d)),
            _resident((d, 3 * d)),
            _resident((d, LANES)),
            _resident((1, LANES)),
            _resident((1, d)),
            _resident((1, d)),
            _resident((d, LANES)),
            _resident((LANES, d)),
            _resident((tm, tm)),
        ],
        out_specs=[
            pl.BlockSpec((1, n_heads, tm, LANES), head),
            pl.BlockSpec((1, n_heads, tm, LANES), head),
            pl.BlockSpec((1, n_heads, 1, VT_ROWS, tm), lambda i, j: (i, 0, j, 0, 0)),
            pl.BlockSpec((1, tm, d), row),
            pl.BlockSpec((1, tm, d), row),
            pl.BlockSpec((1, tm, n_heads), row),
        ],
        out_shape=[
            jax.ShapeDtypeStruct((b, n_heads, t, LANES), BF16),
            jax.ShapeDtypeStruct((b, n_heads, t, LANES), BF16),
            jax.ShapeDtypeStruct((b, n_heads, t // tm, VT_ROWS, tm), BF16),
            jax.ShapeDtypeStruct((b, t, d), F32),
            jax.ShapeDtypeStruct((b, t, d), F32),
            jax.ShapeDtypeStruct((b, t, n_heads), F32),
        ],
        scratch_shapes=[pltpu.VMEM((SUBLANES, LANES), F32)],
        compiler_params=_params("arbitrary", "arbitrary"),
        name="fox_pre",
    )(x, g, w_qkv, w_f, b_f, qn, kn, seg, expand, tri)


def _fox_attn_kernel(qa_ref, ka_ref, vt_ref, o_ref, acc_ref):
    pair, tq = qa_ref.shape[1], qa_ref.shape[2]
    vrows = vt_ref.shape[3]
    qi = pl.program_id(2)
    key_idx = lax.broadcasted_iota(jnp.int32, (tq, tq), 0)
    qry_idx = lax.broadcasted_iota(jnp.int32, (tq, tq), 1)
    acc_ref[...] = jnp.zeros(acc_ref.shape, F32)

    def block(ki, n_blk, m_prev, masked):
        start = pl.multiple_of(ki * tq, tq)
        sts = [_dot_nt(ka_ref[0, j, pl.ds(start, n_blk * tq), :], qa_ref[0, j]) for j in range(pair)]
        if masked:
            sts = [jnp.where(key_idx <= qry_idx, st, MASK_VALUE) for st in sts]
        m_new = [jnp.maximum(m_prev[j], jnp.max(sts[j], axis=0, keepdims=True)) for j in range(pair)]
        pts = [jnp.exp2(sts[j] - m_new[j]).astype(BF16) for j in range(pair)]
        for j in range(pair):
            rows = slice(j * vrows, (j + 1) * vrows)
            pv = _dot(vt_ref[0, j, ki], pts[j][0:tq])
            for b in range(1, n_blk):
                pv = pv + _dot(vt_ref[0, j, ki + b], pts[j][b * tq:(b + 1) * tq])
            acc_ref[rows, :] = jnp.exp2(m_prev[j] - m_new[j]) * acc_ref[rows, :] + pv
        return tuple(m_new)

    init = tuple(jnp.full((1, tq), MASK_VALUE, F32) for _ in range(pair))
    m = lax.fori_loop(0, qi // 2, lambda i, m: block(2 * i, 2, m, False), init)
    m = lax.cond(qi % 2 == 1, lambda m: block(qi - 1, 1, m, False), lambda m: m, m)
    block(qi, 1, m, True)
    outs = []
    for j in range(pair):
        base = j * vrows
        outs.append(acc_ref[base:base + HEAD_DIM, :] / acc_ref[base + HEAD_DIM:base + HEAD_DIM + 1, :])
    o_ref[0] = jnp.concatenate(outs, axis=0).T.astype(BF16)


def _fox_attn(qa, ka, vt):
    b, n_heads, t, _ = qa.shape
    nk, tq = vt.shape[2], vt.shape[4]
    pair = 2
    return pl.pallas_call(
        _fox_attn_kernel,
        grid=(b, n_heads // pair, t // tq),
        in_specs=[
            pl.BlockSpec((1, pair, tq, LANES), lambda i, h, j: (i, h, j, 0)),
            pl.BlockSpec((1, pair, t, LANES), lambda i, h, j: (i, h, 0, 0)),
            pl.BlockSpec((1, pair, nk, vt.shape[3], tq), lambda i, h, j: (i, h, 0, 0, 0)),
        ],
        out_specs=pl.BlockSpec((1, tq, pair * HEAD_DIM), lambda i, h, j: (i, j, h)),
        out_shape=jax.ShapeDtypeStruct((b, t, n_heads * HEAD_DIM), BF16),
        scratch_shapes=[pltpu.VMEM((pair * vt.shape[3], tq), F32)],
        compiler_params=_params("arbitrary", "arbitrary", "arbitrary"),
        name="fox_attn",
    )(qa, ka, vt)


def _scan_rows(a, b, h0):
    rows, w = a.shape
    groups = rows // SUBLANES
    a3 = a.reshape(groups, SUBLANES, w)
    b3 = b.reshape(groups, SUBLANES, w)
    sub = lax.broadcasted_iota(jnp.int32, (groups, SUBLANES, w), 1)
    s = 1
    while s < SUBLANES:
        keep = sub >= s
        b3 = a3 * jnp.where(keep, pltpu.roll(b3, s, 1), 0.0) + b3
        a3 = a3 * jnp.where(keep, pltpu.roll(a3, s, 1), 1.0)
        s *= 2
    last = SUBLANES - 1
    entry = [h0]
    for g in range(groups - 1):
        entry.append(a3[g, last:last + 1, :] * entry[g] + b3[g, last:last + 1, :])
    return jnp.concatenate([a3[g] * entry[g] + b3[g] for g in range(groups)], axis=0)


def _lru_pre_kernel(x_ref, g_ref, win_ref, bin_ref, cw_ref, cb_ref, wai_ref, ba_ref, bi_ref,
                    lam_ref, z_ref, nh_ref, nc_ref, xbuf, hcarry):
    tm = x_ref.shape[1]
    w = cw_ref.shape[1]
    taps = cw_ref.shape[0]

    @pl.when(pl.program_id(1) == 0)
    def _():
        xbuf[0:SUBLANES, :] = jnp.zeros((SUBLANES, w), F32)
        hcarry[...] = jnp.zeros_like(hcarry)

    xn = _rmsnorm(x_ref[0], g_ref[...]).astype(BF16)
    gate = _gelu_tanh(_dot(xn, win_ref[:, 0:w]) + bin_ref[:, 0:w])
    xb = _dot(xn, win_ref[:, w:2 * w]) + bin_ref[:, w:2 * w]
    xbuf[SUBLANES:SUBLANES + tm, :] = xb
    y = cw_ref[taps - 1:taps, :] * xb + cb_ref[...]
    for k in range(1, taps):
        y = y + cw_ref[taps - 1 - k:taps - k, :] * xbuf[SUBLANES - k:SUBLANES - k + tm, :]
    nc_ref[0] = xbuf[SUBLANES + tm - (taps - 1):SUBLANES + tm, :]
    xbuf[0:SUBLANES, :] = xbuf[tm:tm + SUBLANES, :]

    a, bt = _lru_gates(y, wai_ref, ba_ref[...], bi_ref[...], lam_ref[...], w)
    h = _scan_rows(a, bt, hcarry[0:1, :])
    hcarry[0:1, :] = h[tm - 1:tm, :]
    nh_ref[0] = h[tm - 1:tm, :]
    z_ref[0] = (h * gate).astype(BF16)


def _lru_pre(x, g, w_in, b_in, conv_w, conv_b, w_ai, b_a, b_i, lam):
    b, t, d = x.shape
    w = conv_w.shape[1]
    taps = conv_w.shape[0]
    tm = _row_tile(t, LRU_ROW_TILE)
    return pl.pallas_call(
        _lru_pre_kernel,
        grid=(b, t // tm),
        in_specs=[
            pl.BlockSpec((1, tm, d), lambda i, j: (i, j, 0)),
            _resident((1, d)),
            _resident((d, 2 * w)),
            _resident((1, 2 * w)),
            _resident((taps, w)),
            _resident((1, w)),
            _resident((2, w // 2, w)),
            _resident((1, w)),
            _resident((1, w)),
            _resident((1, w)),
        ],
        out_specs=[
            pl.BlockSpec((1, tm, w), lambda i, j: (i, j, 0)),
            pl.BlockSpec((1, 1, w), lambda i, j: (i, 0, 0)),
            pl.BlockSpec((1, taps - 1, w), lambda i, j: (i, 0, 0)),
        ],
        out_shape=[
            jax.ShapeDtypeStruct((b, t, w), BF16),
            jax.ShapeDtypeStruct((b, 1, w), F32),
            jax.ShapeDtypeStruct((b, taps - 1, w), F32),
        ],
        scratch_shapes=[
            pltpu.VMEM((SUBLANES + tm, w), F32),
            pltpu.VMEM((SUBLANES, w), F32),
        ],
        compiler_params=_params("arbitrary", "arbitrary"),
        name="lru_pre",
    )(x, g, w_in, b_in, conv_w, conv_b, w_ai, b_a, b_i, lam)


def _whole(shape):
    n = len(shape)
    return pl.BlockSpec(shape, lambda *_: (0,) * n)


def _single_step_call(kernel, inputs, out_shapes, name):
    return pl.pallas_call(
        kernel,
        grid=(1,),
        in_specs=[_resident(a.shape) for a in inputs],
        out_specs=[_whole(s.shape) for s in out_shapes],
        out_shape=out_shapes,
        compiler_params=_params("arbitrary"),
        name=name,
    )(*inputs)


def _sconv_s_kernel(x_ref, g_ref, win_ref, cw_ref, b0_ref, b1_ref, z_ref, u_ref):
    d = x_ref.shape[1]
    xn = _rmsnorm(x_ref[...], g_ref[...]).astype(BF16)
    bg = _dot(xn, win_ref[:, 0:d])
    u = _dot(xn, win_ref[:, d:2 * d]) * _dot(xn, win_ref[:, 2 * d:3 * d])
    y = cw_ref[0:1, :] * b0_ref[...] + cw_ref[1:2, :] * b1_ref[...] + cw_ref[2:3, :] * u
    z_ref[...] = (bg * y).astype(BF16)
    u_ref[...] = u


def _post_s_kernel(x_ref, z_ref, wo_ref, g_ref, wg_ref, wu_ref, cw_ref, cb_ref, wd_ref,
                   b0_ref, b1_ref, o_ref, gn_ref):
    x1 = x_ref[...] + _dot(z_ref[...], wo_ref[...])
    xn = _rmsnorm(x1, g_ref[...]).astype(BF16)

    def gstore(sl, g):
        gn_ref[:, sl] = g

    def gprev(sl, k):
        return (b1_ref if k == 1 else b0_ref)[:, sl]

    o_ref[...] = _ffn_body(x1, xn, gprev, gstore, wg_ref, wu_ref, cw_ref, cb_ref, wd_ref)


def _fox_pre_s_kernel(x_ref, g_ref, wqkv_ref, wf_ref, bf_ref, qn_ref, kn_ref, seg_ref, exp_ref,
                      q_ref, k_ref, v_ref, lf_ref):
    xn = _rmsnorm(x_ref[...], g_ref[...]).astype(BF16)
    q, k, v, lf = _fox_project(xn, wqkv_ref, wf_ref, bf_ref, qn_ref, kn_ref, seg_ref, exp_ref)
    q_ref[...] = q * (HEAD_DIM ** -0.5)
    k_ref[...] = k
    v_ref[...] = v
    lf_ref[...] = lf


def _lru_s_kernel(x_ref, g_ref, win_ref, bin_ref, cw_ref, cb_ref, wai_ref, ba_ref, bi_ref, lam_ref,
                  h0_ref, c0_ref, c1_ref, c2_ref, z_ref, h_ref, xb_ref):
    w = cw_ref.shape[1]
    xn = _rmsnorm(x_ref[...], g_ref[...]).astype(BF16)
    gate = _gelu_tanh(_dot(xn, win_ref[:, 0:w]) + bin_ref[:, 0:w])
    xb = _dot(xn, win_ref[:, w:2 * w]) + bin_ref[:, w:2 * w]
    y = (cw_ref[0:1, :] * c0_ref[...] + cw_ref[1:2, :] * c1_ref[...] + cw_ref[2:3, :] * c2_ref[...]
         + cw_ref[3:4, :] * xb + cb_ref[...])
    a, bt = _lru_gates(y, wai_ref, ba_ref[...], bi_ref[...], lam_ref[...], w)
    h = a * h0_ref[...] + bt
    h_ref[...] = h
    xb_ref[...] = xb
    z_ref[...] = (h * gate).astype(BF16)


def _decode_attn_kernel(pt_ref, *refs, n_slots):
    k_refs = refs[:n_slots]
    v_refs = refs[n_slots:2 * n_slots]
    lf_refs = refs[2 * n_slots:3 * n_slots]
    (q_ref, kn_ref, qb_ref, vnb_ref, lfn_ref, later_ref,
     o_ref, m_ref, l_ref, acc_ref, lf_after) = refs[3 * n_slots:]
    n_heads = q_ref.shape[1]

    @pl.when(pl.program_id(1) == 0)
    def _():
        lf_after[...] = jnp.zeros_like(lf_after)
        m_ref[...] = jnp.sum(q_ref[0] * kn_ref[0], axis=1, keepdims=True)
        l_ref[...] = jnp.ones_like(l_ref)
        lane = lax.broadcasted_iota(jnp.int32, acc_ref.shape[1:], 1)
        for h in range(n_heads):
            acc_ref[h] = jnp.where(lane == 0, vnb_ref[0, h], 0.0)

    logits = [None] * n_slots
    for i in reversed(range(n_slots)):
        lf = lf_refs[i][...]
        bias = _dot_exact_rhs01(_split3(lf), later_ref[...]) + lf_after[...] + lfn_ref[0]
        lf_after[...] = lf_after[...] + jnp.sum(lf, axis=1, keepdims=True)
        rows = [jnp.sum(k_refs[i][h] * qb_ref[0, h], axis=0, keepdims=True) for h in range(n_heads)]
        logits[i] = jnp.concatenate(rows, axis=0) + bias
    m_prev = m_ref[...]
    m_new = m_prev
    for s in logits:
        m_new = jnp.maximum(m_new, jnp.max(s, axis=1, keepdims=True))
    alpha = jnp.exp(m_prev - m_new)
    probs = [jnp.exp(s - m_new) for s in logits]
    l_new = alpha * l_ref[...]
    for p in probs:
        l_new = l_new + jnp.sum(p, axis=1, keepdims=True)
    l_ref[...] = l_new
    m_ref[...] = m_new
    for h in range(n_heads):
        a = acc_ref[h] * alpha[h:h + 1, :]
        for i in range(n_slots):
            a = a + probs[i][h:h + 1, :] * v_refs[i][h]
        acc_ref[h] = a

    @pl.when(pl.program_id(1) == pl.num_programs(1) - 1)
    def _():
        for h in range(n_heads):
            o_ref[0, h] = jnp.sum(acc_ref[h], axis=1, keepdims=True) / l_ref[h:h + 1, :]


def _decode_attn(k_t, v_t, logf_t, layer, page_table, q3, kn3, qb, vnb, lf_new, later):
    bs, n_pages = page_table.shape
    n_heads, page = k_t.shape[2], k_t.shape[4]
    g = DECODE_PAGES_PER_STEP if n_pages % DECODE_PAGES_PER_STEP == 0 else 1
    n_groups = n_pages // g

    def page_spec(i):
        def idx(b, p, pt):
            return (layer, pt[b, (n_groups - 1 - p) * g + i], 0, 0, 0)
        return pl.BlockSpec((None, None, n_heads, HEAD_DIM, page), idx)

    def logf_spec(i):
        def idx(b, p, pt):
            return (layer, pt[b, (n_groups - 1 - p) * g + i], 0, 0)
        return pl.BlockSpec((None, None, n_heads, page), idx)

    per_seq = pl.BlockSpec((1, n_heads, HEAD_DIM), lambda b, p, pt: (b, 0, 0))
    per_seq_b = pl.BlockSpec((1, n_heads, HEAD_DIM, page), lambda b, p, pt: (b, 0, 0, 0))
    grid_spec = pltpu.PrefetchScalarGridSpec(
        num_scalar_prefetch=1,
        grid=(bs, n_groups),
        in_specs=([page_spec(i) for i in range(g)] + [page_spec(i) for i in range(g)]
                  + [logf_spec(i) for i in range(g)]
                  + [per_seq, per_seq, per_seq_b, per_seq_b,
                     pl.BlockSpec((1, n_heads, 1), lambda b, p, pt: (b, 0, 0)),
                     pl.BlockSpec((page, page), lambda b, p, pt: (0, 0))]),
        out_specs=pl.BlockSpec((1, n_heads, HEAD_DIM, 1), lambda b, p, pt: (b, 0, 0, 0)),
        scratch_shapes=[
            pltpu.VMEM((n_heads, 1), F32),
            pltpu.VMEM((n_heads, 1), F32),
            pltpu.VMEM((n_heads, HEAD_DIM, page), F32),
            pltpu.VMEM((n_heads, 1), F32),
        ],
    )
    return pl.pallas_call(
        functools.partial(_decode_attn_kernel, n_slots=g),
        grid_spec=grid_spec,
        out_shape=jax.ShapeDtypeStruct((bs, n_heads, HEAD_DIM, 1), F32),
        compiler_params=_params("arbitrary", "arbitrary"),
        name="decode_attn",
    )(page_table, *([k_t] * g), *([v_t] * g), *([logf_t] * g), q3, kn3, qb, vnb, lf_new, later)


def _block_diag(w):
    n, c, _ = w.shape
    eye = jnp.eye(n, dtype=w.dtype)
    return (eye[:, None, :, None] * w[:, :, None, :]).reshape(n * c, n * c)


def kernel(x_prompt, x_sample, state_sconv, cache_k, cache_v, cache_logf, page_table, state_lru_h, state_lru_conv, state_ffn_conv, mix_norm, ffn_norm, sc_w_in, sc_conv_w, sc_w_out, fox_w_qkv, fox_w_f, fox_b_f, fox_q_norm, fox_k_norm, fox_w_o, lru_w_in, lru_b_in, lru_conv_w, lru_conv_b, lru_w_a, lru_b_a, lru_w_i, lru_b_i, lru_lambda, lru_w_out, ffn_w_gate, ffn_w_up, ffn_conv_w, ffn_conv_b, ffn_w_down):
    depth, d = mix_norm.shape
    bp, tp, _ = x_prompt.shape
    bs, ts, _ = x_sample.shape
    assert ts == 1, "the sample group carries one new row per sequence"
    n_heads = fox_w_f.shape[-1]
    assert n_heads * HEAD_DIM == d and n_heads % 2 == 0 and 3 * n_heads <= HEAD_DIM
    n_mixers = 3
    f = ffn_w_gate.shape[-1]
    page = cache_k.shape[2]

    xp = x_prompt
    xs = x_sample.reshape(bs, d)
    row2 = lambda v: v.reshape(1, -1)

    lane_head = jnp.arange(d, dtype=jnp.int32) // HEAD_DIM
    seg = (lane_head[:, None] == jnp.arange(LANES, dtype=jnp.int32)[None, :]).astype(BF16)
    expand = seg.T
    tm_fox = _row_tile(tp, ROW_TILE)
    tri = (jnp.arange(tm_fox)[:, None] >= jnp.arange(tm_fox)[None, :]).astype(BF16)
    later = (jnp.arange(page)[:, None] > jnp.arange(page)[None, :]).astype(BF16)
    k_t = jnp.transpose(cache_k, (0, 1, 3, 4, 2))
    v_t = jnp.transpose(cache_v, (0, 1, 3, 4, 2))
    logf_t = jnp.transpose(cache_logf, (0, 1, 3, 2))

    sc_p, sc_s, fc_p, fc_s = [], [], [], []
    k_p, v_p, lf_p, k_s, v_s, lf_s = [], [], [], [], [], []
    lh_p, lh_s, lc_p, lc_s = [], [], [], []

    for i in range(depth):
        j = i // n_mixers
        g_mix = row2(mix_norm[i])
        if i % n_mixers == 0:
            w_in = sc_w_in[j].astype(BF16)
            w_out = sc_w_out[j].astype(BF16)
            zp, nbp = _sconv_pre(xp, g_mix, w_in, sc_conv_w[j])
            zs, us = _single_step_call(
                _sconv_s_kernel,
                [xs, g_mix, w_in, sc_conv_w[j], state_sconv[j, :, 0], state_sconv[j, :, 1]],
                [jax.ShapeDtypeStruct((bs, d), BF16), jax.ShapeDtypeStruct((bs, d), F32)],
                "sconv_sample")
            sc_p.append(nbp)
            sc_s.append(jnp.stack([state_sconv[j, :, 1], us], axis=1))
        elif i % n_mixers == 1:
            w_qkv = fox_w_qkv[j].astype(BF16)
            w_f = jnp.pad(fox_w_f[j], ((0, 0), (0, LANES - n_heads))).astype(BF16)
            b_f = jnp.pad(fox_b_f[j], (0, LANES - n_heads)).reshape(1, LANES)
            qn = row2(jnp.tile(fox_q_norm[j], n_heads))
            kn = row2(jnp.tile(fox_k_norm[j], n_heads))
            w_out = fox_w_o[j].astype(BF16)
            qa, ka, vt, kp, vp, lfp = _fox_pre(xp, g_mix, w_qkv, w_f, b_f, qn, kn, seg, expand, tri, n_heads)
            zp = _fox_attn(qa, ka, vt)
            k_p.append(kp.reshape(bp, tp, n_heads, HEAD_DIM))
            v_p.append(vp.reshape(bp, tp, n_heads, HEAD_DIM))
            lf_p.append(lfp)

            qs, ks, vs, lfs = _single_step_call(
                _fox_pre_s_kernel,
                [xs, g_mix, w_qkv, w_f, b_f, qn, kn, seg, expand],
                [jax.ShapeDtypeStruct((bs, d), F32)] * 3 + [jax.ShapeDtypeStruct((bs, LANES), F32)],
                "fox_pre_sample")
            lfs = lfs[:, :n_heads]
            qs3 = qs.reshape(bs, n_heads, HEAD_DIM)
            ks3 = ks.reshape(bs, n_heads, HEAD_DIM)
            vs3 = vs.reshape(bs, n_heads, HEAD_DIM)
            along_lanes = lambda a: jnp.broadcast_to(a[..., None], (bs, n_heads, HEAD_DIM, page))
            os4 = _decode_attn(k_t, v_t, logf_t, j, page_table, qs3, ks3, along_lanes(qs3), along_lanes(vs3),
                               lfs.reshape(bs, n_heads, 1), later)
            zs = os4.reshape(bs, d).astype(BF16)
            k_s.append(ks3.reshape(bs, 1, n_heads, HEAD_DIM))
            v_s.append(vs3.reshape(bs, 1, n_heads, HEAD_DIM))
            lf_s.append(lfs.reshape(bs, 1, n_heads))
        else:
            w = lru_conv_w.shape[-1]
            w_in = lru_w_in[j].astype(BF16)
            nb = lru_w_a.shape[1]
            assert nb % 2 == 0 and (w // 2) % LANES == 0
            w_ai = jnp.stack([
                jnp.concatenate([_block_diag(lru_w_a[j, k * nb // 2:(k + 1) * nb // 2]),
                                 _block_diag(lru_w_i[j, k * nb // 2:(k + 1) * nb // 2])], axis=1)
                for k in range(2)]).astype(BF16)
            w_out = lru_w_out[j].astype(BF16)
            small = [row2(lru_b_in[j]), lru_conv_w[j], row2(lru_conv_b[j]), w_ai,
                     row2(lru_b_a[j]), row2(lru_b_i[j]), row2(lru_lambda[j])]
            zp, nhp, ncp = _lru_pre(xp, g_mix, w_in, *small)
            zs, nhs, xbs = _single_step_call(
                _lru_s_kernel,
                [xs, g_mix, w_in, *small, state_lru_h[j],
                 state_lru_conv[j, :, 0], state_lru_conv[j, :, 1], state_lru_conv[j, :, 2]],
                [jax.ShapeDtypeStruct((bs, w), BF16), jax.ShapeDtypeStruct((bs, w), F32),
                 jax.ShapeDtypeStruct((bs, w), F32)],
                "lru_sample")
            lh_p.append(nhp.reshape(bp, w))
            lh_s.append(nhs)
            lc_p.append(ncp)
            lc_s.append(jnp.concatenate([state_lru_conv[j, :, 1:], xbs[:, None, :]], axis=1))

        ffn_w = [row2(ffn_norm[i]), ffn_w_gate[i].astype(BF16), ffn_w_up[i].astype(BF16),
                 ffn_conv_w[i], row2(ffn_conv_b[i]), ffn_w_down[i].astype(BF16)]
        xp, nfp = _post(xp, zp, w_out, *ffn_w)
        xs, gs = _single_step_call(
            _post_s_kernel,
            [xs, zs, w_out, *ffn_w, state_ffn_conv[i, :, 0], state_ffn_conv[i, :, 1]],
            [jax.ShapeDtypeStruct((bs, d), F32), jax.ShapeDtypeStruct((bs, f), F32)],
            "post_ffn_sample")
        fc_p.append(nfp)
        fc_s.append(jnp.stack([state_ffn_conv[i, :, 1], gs], axis=1))

    return (xp, xs.reshape(bs, 1, d),
            jnp.stack(sc_p), jnp.stack(sc_s),
            jnp.stack(k_p), jnp.stack(v_p), jnp.stack(lf_p),
            jnp.stack(k_s), jnp.stack(v_s), jnp.stack(lf_s),
            jnp.stack(lh_p), jnp.stack(lh_s),
            jnp.stack(lc_p), jnp.stack(lc_s),
            jnp.stack(fc_p), jnp.stack(fc_s))
```

```python
import functools

import jax
import jax.numpy as jnp
from jax import lax
from jax.experimental import pallas as pl
from jax.experimental.pallas import tpu as pltpu

F32 = jnp.float32
BF16 = jnp.bfloat16

EPS = 1e-6
HEAD_DIM = 64
LRU_C = 8.0
MASK_VALUE = -1e30
LOG2E = 1.4426950408889634
VT_ROWS = HEAD_DIM + 16

SUBLANES = 8
LANES = 128
V7X_VMEM_LIMIT_BYTES = 56 * 1024 * 1024
ROW_TILE = 512
LRU_ROW_TILE = 256
DECODE_PAGES_PER_STEP = 8


def _dot(a, b):
    return jnp.dot(a, b, preferred_element_type=F32)


def _dot_nt(a, b):
    return lax.dot_general(a, b, (((1,), (1,)), ((), ())), preferred_element_type=F32)


def _split2(x):
    hi = x.astype(BF16)
    lo = (x - hi.astype(F32)).astype(BF16)
    return hi, lo


def _split3(x):
    hi = x.astype(BF16)
    r = x - hi.astype(F32)
    mid = r.astype(BF16)
    lo = (r - mid.astype(F32)).astype(BF16)
    return hi, mid, lo


def _dot_exact_rhs01(x_parts, m01):
    out = _dot(x_parts[0], m01)
    for p in x_parts[1:]:
        out = out + _dot(p, m01)
    return out


def _rmsnorm(x, g):
    ms = jnp.mean(x * x, axis=-1, keepdims=True)
    return x * lax.rsqrt(ms + EPS) * g


def _sigmoid(x):
    return 1.0 / (1.0 + jnp.exp(-x))


def _softplus(x):
    return jnp.maximum(x, 0.0) + jnp.log1p(jnp.exp(-jnp.abs(x)))


def _gelu_tanh(x):
    c2 = 2.0 * 0.7978845608028654
    return x * _sigmoid(c2 * (x + 0.044715 * (x * x * x)))


def _resident(shape):
    n = len(shape)
    return pl.BlockSpec(shape, lambda *_: (0,) * n, pipeline_mode=pl.Buffered(1))


def _layer_spec(stacked, layer):
    tail = stacked.shape[1:]
    zeros = (0,) * len(tail)
    return pl.BlockSpec((None,) + tail, lambda *_: (layer,) + zeros, pipeline_mode=pl.Buffered(1))


def _split_layered(operands):
    arrays, specs = [], []
    for op in operands:
        if isinstance(op, tuple):
            stacked, layer = op
            if stacked.ndim == 2:
                stacked = stacked[:, None, :]
            arrays.append(stacked)
            specs.append(_layer_spec(stacked, layer))
        else:
            arrays.append(op)
            specs.append(_resident(op.shape))
    return arrays, specs


def _params(*sem):
    return pltpu.CompilerParams(dimension_semantics=sem, vmem_limit_bytes=V7X_VMEM_LIMIT_BYTES)


def _row_tile(t, pref):
    return pref if t % pref == 0 else t


def _ffn_chunks(f):
    for n in (1,):
        if f % (n * LANES) == 0:
            return n
    return 1


def _head_rmsnorm(x, gain_tiled, seg, expand):
    ss = _dot_exact_rhs01(_split2(x * x), seg)
    inv = lax.rsqrt(ss * (1.0 / HEAD_DIM) + EPS)
    return x * _dot_exact_rhs01(_split2(inv), expand) * gain_tiled


def _log_sigmoid(x):
    return -_softplus(-x)


def _lru_gates(y, wai_ref, ba, bi, lam, w):
    yb = y.astype(BF16)
    half = w // 2
    parts = [_dot(yb[:, k * half:(k + 1) * half], wai_ref[k]) for k in range(2)]
    r = _sigmoid(jnp.concatenate([pt[:, 0:half] for pt in parts], axis=1) + ba)
    i = _sigmoid(jnp.concatenate([pt[:, half:2 * half] for pt in parts], axis=1) + bi)
    log_a = (-LRU_C) * r * _softplus(-lam)
    a = jnp.exp(log_a)
    b = jnp.sqrt(1.0 - a * a) * (i * y)
    return a, b


def _sconv_pre_kernel(x_ref, g_ref, win_ref, cw_ref, z_ref, nb_ref, ubuf):
    tm, d = x_ref.shape[1], x_ref.shape[2]

    @pl.when(pl.program_id(1) == 0)
    def _():
        ubuf[0:SUBLANES, :] = jnp.zeros((SUBLANES, d), F32)

    xn = _rmsnorm(x_ref[0], g_ref[...]).astype(BF16)
    bg = _dot(xn, win_ref[:, 0:d])
    u = _dot(xn, win_ref[:, d:2 * d]) * _dot(xn, win_ref[:, 2 * d:3 * d])
    ubuf[SUBLANES:SUBLANES + tm, :] = u
    y = (cw_ref[0:1, :] * ubuf[SUBLANES - 2:SUBLANES - 2 + tm, :]
         + cw_ref[1:2, :] * ubuf[SUBLANES - 1:SUBLANES - 1 + tm, :]
         + cw_ref[2:3, :] * u)
    z_ref[0] = (bg * y).astype(BF16)
    nb_ref[0] = ubuf[SUBLANES + tm - 2:SUBLANES + tm, :]
    ubuf[0:SUBLANES, :] = ubuf[tm:tm + SUBLANES, :]


def _sconv_pre(x, weights):
    b, t, d = x.shape
    tm = _row_tile(t, ROW_TILE)
    arrays, specs = _split_layered(weights)
    return pl.pallas_call(
        _sconv_pre_kernel,
        grid=(b, t // tm),
        in_specs=[pl.BlockSpec((1, tm, d), lambda i, j: (i, j, 0))] + specs,
        out_specs=[
            pl.BlockSpec((1, tm, d), lambda i, j: (i, j, 0)),
            pl.BlockSpec((1, 2, d), lambda i, j: (i, 0, 0)),
        ],
        out_shape=[
            jax.ShapeDtypeStruct((b, t, d), BF16),
            jax.ShapeDtypeStruct((b, 2, d), F32),
        ],
        scratch_shapes=[pltpu.VMEM((SUBLANES + tm, d), F32)],
        compiler_params=_params("arbitrary", "arbitrary"),
        name="sconv_pre",
    )(x, *arrays)


def _ffn_body(x1, xn, gprev_fn, gstore_fn, wg_ref, wu_ref, cw_ref, cb_ref, wd_ref):
    f = wg_ref.shape[1]
    n_chunks = _ffn_chunks(f)
    fc = f // n_chunks
    acc = x1
    for c in range(n_chunks):
        sl = slice(c * fc, (c + 1) * fc)
        g = _dot(xn, wg_ref[:, sl])
        gstore_fn(sl, g)
        gc = (cw_ref[0:1, sl] * gprev_fn(sl, 2) + cw_ref[1:2, sl] * gprev_fn(sl, 1)
              + cw_ref[2:3, sl] * g + cb_ref[:, sl])
        up = _dot(xn, wu_ref[:, sl])
        hmid = (gc * _sigmoid(gc) * up).astype(BF16)
        acc = acc + _dot(hmid, wd_ref[sl, :])
    return acc


def _post_kernel(x_ref, z_ref, wo_ref, g_ref, wg_ref, wu_ref, cw_ref, cb_ref, wd_ref,
                 o_ref, nb_ref, gbuf):
    tm = x_ref.shape[1]
    f = wg_ref.shape[1]

    @pl.when(pl.program_id(1) == 0)
    def _():
        gbuf[0:SUBLANES, :] = jnp.zeros((SUBLANES, f), F32)

    x1 = x_ref[0] + _dot(z_ref[0], wo_ref[...])
    xn = _rmsnorm(x1, g_ref[...]).astype(BF16)

    def gstore(sl, g):
        gbuf[SUBLANES:SUBLANES + tm, sl] = g

    def gprev(sl, k):
        return gbuf[SUBLANES - k:SUBLANES - k + tm, sl]

    o_ref[0] = _ffn_body(x1, xn, gprev, gstore, wg_ref, wu_ref, cw_ref, cb_ref, wd_ref)
    nb_ref[0] = gbuf[SUBLANES + tm - 2:SUBLANES + tm, :]
    gbuf[0:SUBLANES, :] = gbuf[tm:tm + SUBLANES, :]


def _post(x, z, weights, f):
    b, t, d = x.shape
    kz = z.shape[2]
    tm = _row_tile(t, ROW_TILE)
    arrays, specs = _split_layered(weights)
    return pl.pallas_call(
        _post_kernel,
        grid=(b, t // tm),
        in_specs=[
            pl.BlockSpec((1, tm, d), lambda i, j: (i, j, 0)),
            pl.BlockSpec((1, tm, kz), lambda i, j: (i, j, 0)),
        ] + specs,
        out_specs=[
            pl.BlockSpec((1, tm, d), lambda i, j: (i, j, 0)),
            pl.BlockSpec((1, 2, f), lambda i, j: (i, 0, 0)),
        ],
        out_shape=[
            jax.ShapeDtypeStruct((b, t, d), F32),
            jax.ShapeDtypeStruct((b, 2, f), F32),
        ],
        scratch_shapes=[pltpu.VMEM((SUBLANES + tm, f), F32)],
        compiler_params=_params("arbitrary", "arbitrary"),
        name="post_ffn",
    )(x, z, *arrays)


def _fox_project(xn, wqkv_ref, wf_ref, bf_ref, qn_ref, kn_ref, seg_ref, exp_ref):
    d = xn.shape[1]
    q = _head_rmsnorm(_dot(xn, wqkv_ref[:, 0:d]), qn_ref[...], seg_ref[...], exp_ref[...])
    k = _head_rmsnorm(_dot(xn, wqkv_ref[:, d:2 * d]), kn_ref[...], seg_ref[...], exp_ref[...])
    v = _dot(xn, wqkv_ref[:, 2 * d:3 * d])
    lf = _log_sigmoid(_dot(xn, wf_ref[...]) + bf_ref[...])
    return q, k, v, lf


def _dot_exact_rhs01_lhs(m01, x_parts):
    out = _dot(m01, x_parts[0])
    for p in x_parts[1:]:
        out = out + _dot(m01, p)
    return out


def _fox_pre_kernel(x_ref, g_ref, wqkv_ref, wf_ref, bf_ref, qn_ref, kn_ref, seg_ref, exp_ref,
                    tri_ref, qa_ref, ka_ref, vt_ref, k_ref, v_ref, lf_ref, carry):
    tm = x_ref.shape[1]
    n_heads = lf_ref.shape[2]

    @pl.when(pl.program_id(1) == 0)
    def _():
        carry[...] = jnp.zeros_like(carry)

    xn = _rmsnorm(x_ref[0], g_ref[...]).astype(BF16)
    q, k, v, lf = _fox_project(xn, wqkv_ref, wf_ref, bf_ref, qn_ref, kn_ref, seg_ref, exp_ref)
    k_ref[0] = k
    v_ref[0] = v
    lf_ref[0] = lf[:, 0:n_heads]
    c = _dot_exact_rhs01_lhs(tri_ref[...], _split3(lf)) + carry[0:1, :]
    carry[0:1, :] = c[tm - 1:tm, :]

    lane = lax.broadcasted_iota(jnp.int32, (tm, LANES), 1)
    lane_row = lax.broadcasted_iota(jnp.int32, (1, LANES), 1)
    hi, mid, lo = (part.astype(F32) for part in _split3(c * LOG2E))
    bias_lo = jnp.where(lane < n_heads, hi,
                        jnp.where(lane < 2 * n_heads, pltpu.roll(mid, n_heads, 1),
                                  jnp.where(lane < 3 * n_heads, pltpu.roll(lo, 2 * n_heads, 1), 0.0)))
    bias_hi = pltpu.roll(bias_lo, HEAD_DIM, 1)
    lower = lane < HEAD_DIM
    qs = q * (HEAD_DIM ** -0.5 * LOG2E)
    for h in range(n_heads):
        col = slice((h // 2) * LANES, (h // 2 + 1) * LANES)
        base = h + (HEAD_DIM if h % 2 == 0 else 0)
        pick = jnp.where(lane_row == base, -1.0, 0.0)
        pick = pick + jnp.where(lane_row == base + n_heads, -1.0, 0.0)
        pick = pick + jnp.where(lane_row == base + 2 * n_heads, -1.0, 0.0)
        if h % 2 == 0:
            qa = jnp.where(lower, qs[:, col], pick)
            ka = jnp.where(lower, k[:, col], bias_hi)
        else:
            qa = jnp.where(lower, pick, qs[:, col])
            ka = jnp.where(lower, bias_lo, k[:, col])
        qa_ref[0, h] = qa.astype(BF16)
        ka_ref[0, h] = ka.astype(BF16)
    vt_ref[0, :, 0, 0:HEAD_DIM, :] = v.T.reshape(n_heads, HEAD_DIM, tm).astype(BF16)
    pad_rows = vt_ref.shape[3] - HEAD_DIM
    first = lax.broadcasted_iota(jnp.int32, (n_heads, pad_rows, tm), 1) == 0
    vt_ref[0, :, 0, HEAD_DIM:HEAD_DIM + pad_rows, :] = jnp.where(first, 1.0, 0.0).astype(BF16)


def _fox_pre(x, weights, tm, n_heads):
    b, t, d = x.shape
    row = lambda i, j: (i, j, 0)
    head = lambda i, j: (i, 0, j, 0)
    arrays, specs = _split_layered(weights)
    return pl.pallas_call(
        _fox_pre_kernel,
        grid=(b, t // tm),
        in_specs=[pl.BlockSpec((1, tm, d), row)] + specs,
        out_specs=[
            pl.BlockSpec((1, n_heads, tm, LANES), head),
            pl.BlockSpec((1, n_heads, tm, LANES), head),
            pl.BlockSpec((1, n_heads, 1, VT_ROWS, tm), lambda i, j: (i, 0, j, 0, 0)),
            pl.BlockSpec((1, tm, d), row),
            pl.BlockSpec((1, tm, d), row),
            pl.BlockSpec((1, tm, n_heads), row),
        ],
        out_shape=[
            jax.ShapeDtypeStruct((b, n_heads, t, LANES), BF16),
            jax.ShapeDtypeStruct((b, n_heads, t, LANES), BF16),
            jax.ShapeDtypeStruct((b, n_heads, t // tm, VT_ROWS, tm), BF16),
            jax.ShapeDtypeStruct((b, t, d), F32),
            jax.ShapeDtypeStruct((b, t, d), F32),
            jax.ShapeDtypeStruct((b, t, n_heads), F32),
        ],
        scratch_shapes=[pltpu.VMEM((SUBLANES, LANES), F32)],
        compiler_params=_params("arbitrary", "arbitrary"),
        name="fox_pre",
    )(x, *arrays)


def _fox_attn_kernel(qa_ref, ka_ref, vt_ref, o_ref, acc_ref, st_a, st_b):
    pair, tq = qa_ref.shape[1], qa_ref.shape[2]
    vrows = vt_ref.shape[3]
    qi = pl.program_id(2)
    acc_ref[...] = jnp.zeros(acc_ref.shape, F32)

    def logits_into(st_ref, ki):
        start = pl.multiple_of(ki * tq, tq)
        for j in range(pair):
            st_ref[j] = _dot_nt(ka_ref[0, j, pl.ds(start, tq), :], qa_ref[0, j])

    def consume(st_ref, ki, m_prev, masked):
        m_out = []
        for j in range(pair):
            st = st_ref[j]
            if masked:
                key_idx = lax.broadcasted_iota(jnp.int32, (tq, tq), 0)
                qry_idx = lax.broadcasted_iota(jnp.int32, (tq, tq), 1)
                st = jnp.where(key_idx <= qry_idx, st, MASK_VALUE)
            m_new = jnp.maximum(m_prev[j], jnp.max(st, axis=0, keepdims=True))
            pt = jnp.exp2(st - m_new).astype(BF16)
            rows = slice(j * vrows, (j + 1) * vrows)
            acc_ref[rows, :] = jnp.exp2(m_prev[j] - m_new) * acc_ref[rows, :] + _dot(vt_ref[0, j, ki], pt)
            m_out.append(m_new)
        return tuple(m_out)

    odd = qi % 2
    m = tuple(jnp.full((1, tq), MASK_VALUE, F32) for _ in range(pair))

    def leading_block(m):
        logits_into(st_b, 0)
        return consume(st_b, 0, m, False)

    m = lax.cond(odd == 1, leading_block, lambda m: m, m)
    logits_into(st_a, odd)

    def two_blocks(i, m):
        k0 = odd + 2 * i
        logits_into(st_b, k0 + 1)
        m = consume(st_a, k0, m, False)
        logits_into(st_a, k0 + 2)
        return consume(st_b, k0 + 1, m, False)

    m = lax.fori_loop(0, qi // 2, two_blocks, m)
    consume(st_a, qi, m, True)
    outs = []
    for j in range(pair):
        base = j * vrows
        outs.append(acc_ref[base:base + HEAD_DIM, :] / acc_ref[base + HEAD_DIM:base + HEAD_DIM + 1, :])
    o_ref[0] = jnp.concatenate(outs, axis=0).T.astype(BF16)


def _fox_attn(qa, ka, vt):
    b, n_heads, t, _ = qa.shape
    nk, tq = vt.shape[2], vt.shape[4]
    pair = 2
    return pl.pallas_call(
        _fox_attn_kernel,
        grid=(b, n_heads // pair, t // tq),
        in_specs=[
            pl.BlockSpec((1, pair, tq, LANES), lambda i, h, j: (i, h, j, 0)),
            pl.BlockSpec((1, pair, t, LANES), lambda i, h, j: (i, h, 0, 0)),
            pl.BlockSpec((1, pair, nk, vt.shape[3], tq), lambda i, h, j: (i, h, 0, 0, 0)),
        ],
        out_specs=pl.BlockSpec((1, tq, pair * HEAD_DIM), lambda i, h, j: (i, j, h)),
        out_shape=jax.ShapeDtypeStruct((b, t, n_heads * HEAD_DIM), BF16),
        scratch_shapes=[
            pltpu.VMEM((pair * vt.shape[3], tq), F32),
            pltpu.VMEM((pair, tq, tq), F32),
            pltpu.VMEM((pair, tq, tq), F32),
        ],
        compiler_params=_params("arbitrary", "arbitrary", "arbitrary"),
        name="fox_attn",
    )(qa, ka, vt)


def _scan_rows(a, b, h0):
    rows, w = a.shape
    groups = rows // SUBLANES
    a3 = a.reshape(groups, SUBLANES, w)
    b3 = b.reshape(groups, SUBLANES, w)
    sub = lax.broadcasted_iota(jnp.int32, (groups, SUBLANES, w), 1)
    s = 1
    while s < SUBLANES:
        keep = sub >= s
        b3 = a3 * jnp.where(keep, pltpu.roll(b3, s, 1), 0.0) + b3
        a3 = a3 * jnp.where(keep, pltpu.roll(a3, s, 1), 1.0)
        s *= 2
    last = SUBLANES - 1
    entry = [h0]
    for g in range(groups - 1):
        entry.append(a3[g, last:last + 1, :] * entry[g] + b3[g, last:last + 1, :])
    return jnp.concatenate([a3[g] * entry[g] + b3[g] for g in range(groups)], axis=0)


def _lru_pre_kernel(x_ref, g_ref, win_ref, bin_ref, cw_ref, cb_ref, wai_ref, ba_ref, bi_ref,
                    lam_ref, z_ref, nh_ref, nc_ref, xbuf, hcarry):
    tm = x_ref.shape[1]
    w = cw_ref.shape[1]
    taps = cw_ref.shape[0]

    @pl.when(pl.program_id(1) == 0)
    def _():
        xbuf[0:SUBLANES, :] = jnp.zeros((SUBLANES, w), F32)
        hcarry[...] = jnp.zeros_like(hcarry)

    xn = _rmsnorm(x_ref[0], g_ref[...]).astype(BF16)
    gate = _gelu_tanh(_dot(xn, win_ref[:, 0:w]) + bin_ref[:, 0:w])
    xb = _dot(xn, win_ref[:, w:2 * w]) + bin_ref[:, w:2 * w]
    xbuf[SUBLANES:SUBLANES + tm, :] = xb
    y = cw_ref[taps - 1:taps, :] * xb + cb_ref[...]
    for k in range(1, taps):
        y = y + cw_ref[taps - 1 - k:taps - k, :] * xbuf[SUBLANES - k:SUBLANES - k + tm, :]
    nc_ref[0] = xbuf[SUBLANES + tm - (taps - 1):SUBLANES + tm, :]
    xbuf[0:SUBLANES, :] = xbuf[tm:tm + SUBLANES, :]

    a, bt = _lru_gates(y, wai_ref, ba_ref[...], bi_ref[...], lam_ref[...], w)
    h = _scan_rows(a, bt, hcarry[0:1, :])
    hcarry[0:1, :] = h[tm - 1:tm, :]
    nh_ref[0] = h[tm - 1:tm, :]
    z_ref[0] = (h * gate).astype(BF16)


def _lru_pre(x, weights, w, taps):
    b, t, d = x.shape
    tm = _row_tile(t, LRU_ROW_TILE)
    arrays, specs = _split_layered(weights)
    return pl.pallas_call(
        _lru_pre_kernel,
        grid=(b, t // tm),
        in_specs=[pl.BlockSpec((1, tm, d), lambda i, j: (i, j, 0))] + specs,
        out_specs=[
            pl.BlockSpec((1, tm, w), lambda i, j: (i, j, 0)),
            pl.BlockSpec((1, 1, w), lambda i, j: (i, 0, 0)),
            pl.BlockSpec((1, taps - 1, w), lambda i, j: (i, 0, 0)),
        ],
        out_shape=[
            jax.ShapeDtypeStruct((b, t, w), BF16),
            jax.ShapeDtypeStruct((b, 1, w), F32),
            jax.ShapeDtypeStruct((b, taps - 1, w), F32),
        ],
        scratch_shapes=[
            pltpu.VMEM((SUBLANES + tm, w), F32),
            pltpu.VMEM((SUBLANES, w), F32),
        ],
        compiler_params=_params("arbitrary", "arbitrary"),
        name="lru_pre",
    )(x, *arrays)


def _whole(shape):
    n = len(shape)
    return pl.BlockSpec(shape, lambda *_: (0,) * n)


def _single_step_call(kernel, inputs, out_shapes, name):
    inputs, specs = _split_layered(inputs)
    return pl.pallas_call(
        kernel,
        grid=(1,),
        in_specs=specs,
        out_specs=[_whole(s.shape) for s in out_shapes],
        out_shape=out_shapes,
        compiler_params=_params("arbitrary"),
        name=name,
    )(*inputs)


def _sconv_s_kernel(x_ref, g_ref, win_ref, cw_ref, b0_ref, b1_ref, z_ref, u_ref):
    d = x_ref.shape[1]
    xn = _rmsnorm(x_ref[...], g_ref[...]).astype(BF16)
    bg = _dot(xn, win_ref[:, 0:d])
    u = _dot(xn, win_ref[:, d:2 * d]) * _dot(xn, win_ref[:, 2 * d:3 * d])
    y = cw_ref[0:1, :] * b0_ref[...] + cw_ref[1:2, :] * b1_ref[...] + cw_ref[2:3, :] * u
    z_ref[...] = (bg * y).astype(BF16)
    u_ref[...] = u


def _post_s_kernel(x_ref, z_ref, wo_ref, g_ref, wg_ref, wu_ref, cw_ref, cb_ref, wd_ref,
                   b0_ref, b1_ref, o_ref, gn_ref):
    x1 = x_ref[...] + _dot(z_ref[...], wo_ref[...])
    xn = _rmsnorm(x1, g_ref[...]).astype(BF16)

    def gstore(sl, g):
        gn_ref[:, sl] = g

    def gprev(sl, k):
        return (b1_ref if k == 1 else b0_ref)[:, sl]

    o_ref[...] = _ffn_body(x1, xn, gprev, gstore, wg_ref, wu_ref, cw_ref, cb_ref, wd_ref)


def _fox_pre_s_kernel(x_ref, g_ref, wqkv_ref, wf_ref, bf_ref, qn_ref, kn_ref, seg_ref, exp_ref,
                      q_ref, k_ref, v_ref, lf_ref):
    xn = _rmsnorm(x_ref[...], g_ref[...]).astype(BF16)
    q, k, v, lf = _fox_project(xn, wqkv_ref, wf_ref, bf_ref, qn_ref, kn_ref, seg_ref, exp_ref)
    q_ref[...] = q * (HEAD_DIM ** -0.5)
    k_ref[...] = k
    v_ref[...] = v
    lf_ref[...] = lf


def _lru_s_kernel(x_ref, g_ref, win_ref, bin_ref, cw_ref, cb_ref, wai_ref, ba_ref, bi_ref, lam_ref,
                  h0_ref, c0_ref, c1_ref, c2_ref, z_ref, h_ref, xb_ref):
    w = cw_ref.shape[1]
    xn = _rmsnorm(x_ref[...], g_ref[...]).astype(BF16)
    gate = _gelu_tanh(_dot(xn, win_ref[:, 0:w]) + bin_ref[:, 0:w])
    xb = _dot(xn, win_ref[:, w:2 * w]) + bin_ref[:, w:2 * w]
    y = (cw_ref[0:1, :] * c0_ref[...] + cw_ref[1:2, :] * c1_ref[...] + cw_ref[2:3, :] * c2_ref[...]
         + cw_ref[3:4, :] * xb + cb_ref[...])
    a, bt = _lru_gates(y, wai_ref, ba_ref[...], bi_ref[...], lam_ref[...], w)
    h = a * h0_ref[...] + bt
    h_ref[...] = h
    xb_ref[...] = xb
    z_ref[...] = (h * gate).astype(BF16)


def _decode_attn_kernel(pt_ref, *refs, n_slots):
    k_refs = refs[:n_slots]
    v_refs = refs[n_slots:2 * n_slots]
    lf_refs = refs[2 * n_slots:3 * n_slots]
    (q_ref, kn_ref, qb_ref, vnb_ref, lfn_ref, later_ref,
     o_ref, m_ref, l_ref, acc_ref, lf_after) = refs[3 * n_slots:]
    n_heads = q_ref.shape[1]

    @pl.when(pl.program_id(1) == 0)
    def _():
        lf_after[...] = jnp.zeros_like(lf_after)
        m_ref[...] = jnp.sum(q_ref[0] * kn_ref[0], axis=1, keepdims=True)
        l_ref[...] = jnp.ones_like(l_ref)
        lane = lax.broadcasted_iota(jnp.int32, acc_ref.shape[1:], 1)
        for h in range(n_heads):
            acc_ref[h] = jnp.where(lane == 0, vnb_ref[0, h], 0.0)

    logits = [None] * n_slots
    for i in reversed(range(n_slots)):
        lf = lf_refs[i][...]
        bias = _dot_exact_rhs01(_split3(lf), later_ref[...]) + lf_after[...] + lfn_ref[0]
        lf_after[...] = lf_after[...] + jnp.sum(lf, axis=1, keepdims=True)
        rows = [jnp.sum(k_refs[i][h] * qb_ref[0, h], axis=0, keepdims=True) for h in range(n_heads)]
        logits[i] = jnp.concatenate(rows, axis=0) + bias
    m_prev = m_ref[...]
    m_new = m_prev
    for s in logits:
        m_new = jnp.maximum(m_new, jnp.max(s, axis=1, keepdims=True))
    alpha = jnp.exp(m_prev - m_new)
    probs = [jnp.exp(s - m_new) for s in logits]
    l_new = alpha * l_ref[...]
    for p in probs:
        l_new = l_new + jnp.sum(p, axis=1, keepdims=True)
    l_ref[...] = l_new
    m_ref[...] = m_new
    for h in range(n_heads):
        a = acc_ref[h] * alpha[h:h + 1, :]
        for i in range(n_slots):
            a = a + probs[i][h:h + 1, :] * v_refs[i][h]
        acc_ref[h] = a

    @pl.when(pl.program_id(1) == pl.num_programs(1) - 1)
    def _():
        for h in range(n_heads):
            o_ref[0, h] = jnp.sum(acc_ref[h], axis=1, keepdims=True) / l_ref[h:h + 1, :]


def _decode_attn(k_t, v_t, logf_t, layer, page_table, q3, kn3, qb, vnb, lf_new, later):
    bs, n_pages = page_table.shape
    n_heads, page = k_t.shape[2], k_t.shape[4]
    g = DECODE_PAGES_PER_STEP if n_pages % DECODE_PAGES_PER_STEP == 0 else 1
    n_groups = n_pages // g

    def page_spec(i):
        def idx(b, p, pt):
            return (layer, pt[b, (n_groups - 1 - p) * g + i], 0, 0, 0)
        return pl.BlockSpec((None, None, n_heads, HEAD_DIM, page), idx)

    def logf_spec(i):
        def idx(b, p, pt):
            return (layer, pt[b, (n_groups - 1 - p) * g + i], 0, 0)
        return pl.BlockSpec((None, None, n_heads, page), idx)

    per_seq = pl.BlockSpec((1, n_heads, HEAD_DIM), lambda b, p, pt: (b, 0, 0))
    per_seq_b = pl.BlockSpec((1, n_heads, HEAD_DIM, page), lambda b, p, pt: (b, 0, 0, 0))
    grid_spec = pltpu.PrefetchScalarGridSpec(
        num_scalar_prefetch=1,
        grid=(bs, n_groups),
        in_specs=([page_spec(i) for i in range(g)] + [page_spec(i) for i in range(g)]
                  + [logf_spec(i) for i in range(g)]
                  + [per_seq, per_seq, per_seq_b, per_seq_b,
                     pl.BlockSpec((1, n_heads, 1), lambda b, p, pt: (b, 0, 0)),
                     pl.BlockSpec((page, page), lambda b, p, pt: (0, 0))]),
        out_specs=pl.BlockSpec((1, n_heads, HEAD_DIM, 1), lambda b, p, pt: (b, 0, 0, 0)),
        scratch_shapes=[
            pltpu.VMEM((n_heads, 1), F32),
            pltpu.VMEM((n_heads, 1), F32),
            pltpu.VMEM((n_heads, HEAD_DIM, page), F32),
            pltpu.VMEM((n_heads, 1), F32),
        ],
    )
    return pl.pallas_call(
        functools.partial(_decode_attn_kernel, n_slots=g),
        grid_spec=grid_spec,
        out_shape=jax.ShapeDtypeStruct((bs, n_heads, HEAD_DIM, 1), F32),
        compiler_params=_params("arbitrary", "arbitrary"),
        name="decode_attn",
    )(page_table, *([k_t] * g), *([v_t] * g), *([logf_t] * g), q3, kn3, qb, vnb, lf_new, later)


def _block_diag(w):
    n, c, _ = w.shape
    eye = jnp.eye(n, dtype=w.dtype)
    return (eye[:, None, :, None] * w[:, :, None, :]).reshape(n * c, n * c)


def kernel(x_prompt, x_sample, state_sconv, cache_k, cache_v, cache_logf, page_table, state_lru_h, state_lru_conv, state_ffn_conv, mix_norm, ffn_norm, sc_w_in, sc_conv_w, sc_w_out, fox_w_qkv, fox_w_f, fox_b_f, fox_q_norm, fox_k_norm, fox_w_o, lru_w_in, lru_b_in, lru_conv_w, lru_conv_b, lru_w_a, lru_b_a, lru_w_i, lru_b_i, lru_lambda, lru_w_out, ffn_w_gate, ffn_w_up, ffn_conv_w, ffn_conv_b, ffn_w_down):
    depth, d = mix_norm.shape
    bp, tp, _ = x_prompt.shape
    bs, ts, _ = x_sample.shape
    assert ts == 1, "the sample group carries one new row per sequence"
    n_heads = fox_w_f.shape[-1]
    assert n_heads * HEAD_DIM == d and n_heads % 2 == 0 and 3 * n_heads <= HEAD_DIM
    n_mixers = 3
    f = ffn_w_gate.shape[-1]
    page = cache_k.shape[2]

    xp = x_prompt
    xs = x_sample.reshape(bs, d)
    row2 = lambda v: v.reshape(1, -1)

    lane_head = jnp.arange(d, dtype=jnp.int32) // HEAD_DIM
    seg = (lane_head[:, None] == jnp.arange(LANES, dtype=jnp.int32)[None, :]).astype(BF16)
    expand = seg.T
    tm_fox = _row_tile(tp, ROW_TILE)
    tri = (jnp.arange(tm_fox)[:, None] >= jnp.arange(tm_fox)[None, :]).astype(BF16)
    later = (jnp.arange(page)[:, None] > jnp.arange(page)[None, :]).astype(BF16)
    k_t = jnp.transpose(cache_k, (0, 1, 3, 4, 2))
    v_t = jnp.transpose(cache_v, (0, 1, 3, 4, 2))
    logf_t = jnp.transpose(cache_logf, (0, 1, 3, 2))

    bf = lambda w: w.astype(BF16)
    sc_w_in_b, sc_w_out_b = bf(sc_w_in), bf(sc_w_out)
    fox_w_qkv_b, fox_w_o_b = bf(fox_w_qkv), bf(fox_w_o)
    lru_w_in_b, lru_w_out_b = bf(lru_w_in), bf(lru_w_out)
    ffn_w_gate_b, ffn_w_up_b, ffn_w_down_b = bf(ffn_w_gate), bf(ffn_w_up), bf(ffn_w_down)

    sc_p, sc_s, fc_p, fc_s = [], [], [], []
    k_p, v_p, lf_p, k_s, v_s, lf_s = [], [], [], [], [], []
    lh_p, lh_s, lc_p, lc_s = [], [], [], []

    for i in range(depth):
        j = i // n_mixers
        g_mix = (mix_norm, i)
        if i % n_mixers == 0:
            weights = [g_mix, (sc_w_in_b, j), (sc_conv_w, j)]
            w_out = (sc_w_out_b, j)
            zp, nbp = _sconv_pre(xp, weights)
            zs, us = _single_step_call(
                _sconv_s_kernel,
                [xs, *weights, state_sconv[j, :, 0], state_sconv[j, :, 1]],
                [jax.ShapeDtypeStruct((bs, d), BF16), jax.ShapeDtypeStruct((bs, d), F32)],
                "sconv_sample")
            sc_p.append(nbp)
            sc_s.append(jnp.stack([state_sconv[j, :, 1], us], axis=1))
        elif i % n_mixers == 1:
            w_f = jnp.pad(fox_w_f[j], ((0, 0), (0, LANES - n_heads))).astype(BF16)
            b_f = jnp.pad(fox_b_f[j], (0, LANES - n_heads)).reshape(1, LANES)
            qn = row2(jnp.tile(fox_q_norm[j], n_heads))
            kn = row2(jnp.tile(fox_k_norm[j], n_heads))
            weights = [g_mix, (fox_w_qkv_b, j), w_f, b_f, qn, kn, seg, expand]
            w_out = (fox_w_o_b, j)
            qa, ka, vt, kp, vp, lfp = _fox_pre(xp, weights + [tri], tm_fox, n_heads)
            zp = _fox_attn(qa, ka, vt)
            k_p.append(kp.reshape(bp, tp, n_heads, HEAD_DIM))
            v_p.append(vp.reshape(bp, tp, n_heads, HEAD_DIM))
            lf_p.append(lfp)

            qs, ks, vs, lfs = _single_step_call(
                _fox_pre_s_kernel,
                [xs, *weights],
                [jax.ShapeDtypeStruct((bs, d), F32)] * 3 + [jax.ShapeDtypeStruct((bs, LANES), F32)],
                "fox_pre_sample")
            lfs = lfs[:, :n_heads]
            qs3 = qs.reshape(bs, n_heads, HEAD_DIM)
            ks3 = ks.reshape(bs, n_heads, HEAD_DIM)
            vs3 = vs.reshape(bs, n_heads, HEAD_DIM)
            along_lanes = lambda a: jnp.broadcast_to(a[..., None], (bs, n_heads, HEAD_DIM, page))
            os4 = _decode_attn(k_t, v_t, logf_t, j, page_table, qs3, ks3, along_lanes(qs3), along_lanes(vs3),
                               lfs.reshape(bs, n_heads, 1), later)
            zs = os4.reshape(bs, d).astype(BF16)
            k_s.append(ks3.reshape(bs, 1, n_heads, HEAD_DIM))
            v_s.append(vs3.reshape(bs, 1, n_heads, HEAD_DIM))
            lf_s.append(lfs.reshape(bs, 1, n_heads))
        else:
            w = lru_conv_w.shape[-1]
            taps = lru_conv_w.shape[1]
            nb = lru_w_a.shape[1]
            assert nb % 2 == 0 and (w // 2) % LANES == 0
            w_ai = jnp.stack([
                jnp.concatenate([_block_diag(lru_w_a[j, k * nb // 2:(k + 1) * nb // 2]),
                                 _block_diag(lru_w_i[j, k * nb // 2:(k + 1) * nb // 2])], axis=1)
                for k in range(2)]).astype(BF16)
            weights = [g_mix, (lru_w_in_b, j), (lru_b_in, j), (lru_conv_w, j), (lru_conv_b, j), w_ai,
                       (lru_b_a, j), (lru_b_i, j), (lru_lambda, j)]
            w_out = (lru_w_out_b, j)
            zp, nhp, ncp = _lru_pre(xp, weights, w, taps)
            zs, nhs, xbs = _single_step_call(
                _lru_s_kernel,
                [xs, *weights, state_lru_h[j],
                 state_lru_conv[j, :, 0], state_lru_conv[j, :, 1], state_lru_conv[j, :, 2]],
                [jax.ShapeDtypeStruct((bs, w), BF16), jax.ShapeDtypeStruct((bs, w), F32),
                 jax.ShapeDtypeStruct((bs, w), F32)],
                "lru_sample")
            lh_p.append(nhp.reshape(bp, w))
            lh_s.append(nhs)
            lc_p.append(ncp)
            lc_s.append(jnp.concatenate([state_lru_conv[j, :, 1:], xbs[:, None, :]], axis=1))

        post_w = [w_out, (ffn_norm, i), (ffn_w_gate_b, i), (ffn_w_up_b, i), (ffn_conv_w, i), (ffn_conv_b, i),
                  (ffn_w_down_b, i)]
        xp, nfp = _post(xp, zp, post_w, f)
        xs, gs = _single_step_call(
            _post_s_kernel,
            [xs, zs, *post_w, state_ffn_conv[i, :, 0], state_ffn_conv[i, :, 1]],
            [jax.ShapeDtypeStruct((bs, d), F32), jax.ShapeDtypeStruct((bs, f), F32)],
            "post_ffn_sample")
        fc_p.append(nfp)
        fc_s.append(jnp.stack([state_ffn_conv[i, :, 1], gs], axis=1))

    return (xp, xs.reshape(bs, 1, d),
            jnp.stack(sc_p), jnp.stack(sc_s),
            jnp.stack(k_p), jnp.stack(v_p), jnp.stack(lf_p),
            jnp.stack(k_s), jnp.stack(v_s), jnp.stack(lf_s),
            jnp.stack(lh_p), jnp.stack(lh_s),
            jnp.stack(lc_p), jnp.stack(lc_s),
            jnp.stack(fc_p), jnp.stack(fc_s))
```

```python
import functools

import jax
import jax.numpy as jnp
from jax import lax
from jax.experimental import pallas as pl
from jax.experimental.pallas import tpu as pltpu

F32 = jnp.float32
BF16 = jnp.bfloat16

EPS = 1e-6
HEAD_DIM = 64
LRU_C = 8.0
MASK_VALUE = -1e30
LOG2E = 1.4426950408889634
VT_ROWS = HEAD_DIM + 16

SUBLANES = 8
LANES = 128
V7X_VMEM_LIMIT_BYTES = 56 * 1024 * 1024
ROW_TILE = 512
LRU_ROW_TILE = 256
DECODE_PAGES_PER_STEP = 16


def _dot(a, b):
    return jnp.dot(a, b, preferred_element_type=F32)


def _dot_nt(a, b):
    return lax.dot_general(a, b, (((1,), (1,)), ((), ())), preferred_element_type=F32)


def _split2(x):
    hi = x.astype(BF16)
    lo = (x - hi.astype(F32)).astype(BF16)
    return hi, lo


def _split3(x):
    hi = x.astype(BF16)
    r = x - hi.astype(F32)
    mid = r.astype(BF16)
    lo = (r - mid.astype(F32)).astype(BF16)
    return hi, mid, lo


def _dot_exact_rhs01(x_parts, m01):
    out = _dot(x_parts[0], m01)
    for p in x_parts[1:]:
        out = out + _dot(p, m01)
    return out


def _rmsnorm(x, g):
    ms = jnp.mean(x * x, axis=-1, keepdims=True)
    return x * lax.rsqrt(ms + EPS) * g


def _sigmoid(x):
    return 1.0 / (1.0 + jnp.exp2(x * (-LOG2E)))


def _softplus(x):
    return jnp.maximum(x, 0.0) + jnp.log1p(jnp.exp(-jnp.abs(x)))


def _gelu_tanh(x):
    k = -2.0 * LOG2E * 0.7978845608028654
    return x / (1.0 + jnp.exp2(x * (k + (k * 0.044715) * (x * x))))


def _resident(shape):
    n = len(shape)
    return pl.BlockSpec(shape, lambda *_: (0,) * n, pipeline_mode=pl.Buffered(1))


def _layer_spec(stacked, layer):
    tail = stacked.shape[1:]
    zeros = (0,) * len(tail)
    return pl.BlockSpec((None,) + tail, lambda *_: (layer,) + zeros, pipeline_mode=pl.Buffered(1))


def _split_layered(operands):
    arrays, specs = [], []
    for op in operands:
        if isinstance(op, tuple):
            stacked, layer = op
            if stacked.ndim == 2:
                stacked = stacked[:, None, :]
            arrays.append(stacked)
            specs.append(_layer_spec(stacked, layer))
        else:
            arrays.append(op)
            specs.append(_resident(op.shape))
    return arrays, specs


def _params(*sem):
    return pltpu.CompilerParams(dimension_semantics=sem, vmem_limit_bytes=V7X_VMEM_LIMIT_BYTES)


def _row_tile(t, pref):
    return pref if t % pref == 0 else t


def _ffn_chunks(f):
    for n in (1,):
        if f % (n * LANES) == 0:
            return n
    return 1


def _head_rmsnorm(x, gain_tiled, seg, expand):
    ss = _dot_exact_rhs01(_split2(x * x), seg)
    inv = lax.rsqrt(ss * (1.0 / HEAD_DIM) + EPS)
    return x * _dot_exact_rhs01(_split2(inv), expand) * gain_tiled


def _log_sigmoid(x):
    return -_softplus(-x)


def _lru_gates(y, wai_ref, ba, bi, lam, w):
    yb = y.astype(BF16)
    half = w // 2
    parts = [_dot(yb[:, k * half:(k + 1) * half], wai_ref[k]) for k in range(2)]
    r = _sigmoid(jnp.concatenate([pt[:, 0:half] for pt in parts], axis=1) + ba)
    i = _sigmoid(jnp.concatenate([pt[:, half:2 * half] for pt in parts], axis=1) + bi)
    a = jnp.exp2(r * ((-LRU_C * LOG2E) * _softplus(-lam)))
    b = jnp.sqrt(1.0 - a * a) * (i * y)
    return a, b


def _sconv_pre_kernel(x_ref, g_ref, win_ref, cw_ref, z_ref, nb_ref, ubuf):
    tm, d = x_ref.shape[1], x_ref.shape[2]

    @pl.when(pl.program_id(1) == 0)
    def _():
        ubuf[0:SUBLANES, :] = jnp.zeros((SUBLANES, d), F32)

    xn = _rmsnorm(x_ref[0], g_ref[...]).astype(BF16)
    u = _dot(xn, win_ref[:, d:2 * d]) * _dot(xn, win_ref[:, 2 * d:3 * d])
    ubuf[SUBLANES:SUBLANES + tm, :] = u
    y = (cw_ref[0:1, :] * ubuf[SUBLANES - 2:SUBLANES - 2 + tm, :]
         + cw_ref[1:2, :] * ubuf[SUBLANES - 1:SUBLANES - 1 + tm, :]
         + cw_ref[2:3, :] * u)
    z_ref[0] = (_dot(xn, win_ref[:, 0:d]) * y).astype(BF16)
    nb_ref[0] = ubuf[SUBLANES + tm - 2:SUBLANES + tm, :]
    ubuf[0:SUBLANES, :] = ubuf[tm:tm + SUBLANES, :]


def _sconv_pre(x, weights):
    b, t, d = x.shape
    tm = _row_tile(t, ROW_TILE)
    arrays, specs = _split_layered(weights)
    return pl.pallas_call(
        _sconv_pre_kernel,
        grid=(b, t // tm),
        in_specs=[pl.BlockSpec((1, tm, d), lambda i, j: (i, j, 0))] + specs,
        out_specs=[
            pl.BlockSpec((1, tm, d), lambda i, j: (i, j, 0)),
            pl.BlockSpec((1, 2, d), lambda i, j: (i, 0, 0)),
        ],
        out_shape=[
            jax.ShapeDtypeStruct((b, t, d), BF16),
            jax.ShapeDtypeStruct((b, 2, d), F32),
        ],
        scratch_shapes=[pltpu.VMEM((SUBLANES + tm, d), F32)],
        compiler_params=_params("arbitrary", "arbitrary"),
        name="sconv_pre",
    )(x, *arrays)


def _ffn_body(x1, xn, gprev_fn, gstore_fn, wg_ref, wu_ref, cw_ref, cb_ref, wd_ref):
    f = wg_ref.shape[1]
    n_chunks = _ffn_chunks(f)
    fc = f // n_chunks
    acc = x1
    for c in range(n_chunks):
        sl = slice(c * fc, (c + 1) * fc)
        g = _dot(xn, wg_ref[:, sl])
        gstore_fn(sl, g)
        gc = (cw_ref[0:1, sl] * gprev_fn(sl, 2) + cw_ref[1:2, sl] * gprev_fn(sl, 1)
              + cw_ref[2:3, sl] * g + cb_ref[:, sl])
        up = _dot(xn, wu_ref[:, sl])
        hmid = (gc * _sigmoid(gc) * up).astype(BF16)
        acc = acc + _dot(hmid, wd_ref[sl, :])
    return acc


def _post_kernel(x_ref, z_ref, wo_ref, g_ref, wg_ref, wu_ref, cw_ref, cb_ref, wd_ref,
                 o_ref, nb_ref, gbuf):
    tm = x_ref.shape[1]
    f = wg_ref.shape[1]

    @pl.when(pl.program_id(1) == 0)
    def _():
        gbuf[0:SUBLANES, :] = jnp.zeros((SUBLANES, f), F32)

    x1 = x_ref[0] + _dot(z_ref[0], wo_ref[...])
    xn = _rmsnorm(x1, g_ref[...]).astype(BF16)

    def gstore(sl, g):
        gbuf[SUBLANES:SUBLANES + tm, sl] = g

    def gprev(sl, k):
        return gbuf[SUBLANES - k:SUBLANES - k + tm, sl]

    o_ref[0] = _ffn_body(x1, xn, gprev, gstore, wg_ref, wu_ref, cw_ref, cb_ref, wd_ref)
    nb_ref[0] = gbuf[SUBLANES + tm - 2:SUBLANES + tm, :]
    gbuf[0:SUBLANES, :] = gbuf[tm:tm + SUBLANES, :]


def _post(x, z, weights, f):
    b, t, d = x.shape
    kz = z.shape[2]
    tm = _row_tile(t, ROW_TILE)
    arrays, specs = _split_layered(weights)
    return pl.pallas_call(
        _post_kernel,
        grid=(b, t // tm),
        in_specs=[
            pl.BlockSpec((1, tm, d), lambda i, j: (i, j, 0)),
            pl.BlockSpec((1, tm, kz), lambda i, j: (i, j, 0)),
        ] + specs,
        out_specs=[
            pl.BlockSpec((1, tm, d), lambda i, j: (i, j, 0)),
            pl.BlockSpec((1, 2, f), lambda i, j: (i, 0, 0)),
        ],
        out_shape=[
            jax.ShapeDtypeStruct((b, t, d), F32),
            jax.ShapeDtypeStruct((b, 2, f), F32),
        ],
        scratch_shapes=[pltpu.VMEM((SUBLANES + tm, f), F32)],
        compiler_params=_params("arbitrary", "arbitrary"),
        name="post_ffn",
    )(x, z, *arrays)


def _fox_project(xn, wqkv_ref, wf_ref, bf_ref, qn_ref, kn_ref, seg_ref, exp_ref):
    d = xn.shape[1]
    q = _head_rmsnorm(_dot(xn, wqkv_ref[:, 0:d]), qn_ref[...], seg_ref[...], exp_ref[...])
    k = _head_rmsnorm(_dot(xn, wqkv_ref[:, d:2 * d]), kn_ref[...], seg_ref[...], exp_ref[...])
    v = _dot(xn, wqkv_ref[:, 2 * d:3 * d])
    lf = _log_sigmoid(_dot(xn, wf_ref[...]) + bf_ref[...])
    return q, k, v, lf


def _dot_exact_rhs01_lhs(m01, x_parts):
    out = _dot(m01, x_parts[0])
    for p in x_parts[1:]:
        out = out + _dot(m01, p)
    return out


def _fox_pre_kernel(x_ref, g_ref, wqkv_ref, wf_ref, bf_ref, qn_ref, kn_ref, seg_ref, exp_ref,
                    tri_ref, qa_ref, ka_ref, vt_ref, k_ref, v_ref, lf_ref, carry):
    tm = x_ref.shape[1]
    n_heads = lf_ref.shape[2]

    @pl.when(pl.program_id(1) == 0)
    def _():
        carry[...] = jnp.zeros_like(carry)

    xn = _rmsnorm(x_ref[0], g_ref[...]).astype(BF16)
    d = xn.shape[1]
    v = _dot(xn, wqkv_ref[:, 2 * d:3 * d])
    v_ref[0] = v
    vt_ref[0, :, 0, 0:HEAD_DIM, :] = v.T.reshape(n_heads, HEAD_DIM, tm).astype(BF16)
    pad_rows = vt_ref.shape[3] - HEAD_DIM
    first = lax.broadcasted_iota(jnp.int32, (n_heads, pad_rows, tm), 1) == 0
    vt_ref[0, :, 0, HEAD_DIM:HEAD_DIM + pad_rows, :] = jnp.where(first, 1.0, 0.0).astype(BF16)

    lf = _log_sigmoid(_dot(xn, wf_ref[...]) + bf_ref[...])
    lf_ref[0] = lf[:, 0:n_heads]
    c = _dot_exact_rhs01_lhs(tri_ref[...], _split3(lf)) + carry[0:1, :]
    carry[0:1, :] = c[tm - 1:tm, :]
    lane = lax.broadcasted_iota(jnp.int32, (tm, LANES), 1)
    lane_row = lax.broadcasted_iota(jnp.int32, (1, LANES), 1)
    hi, mid, lo = (part.astype(F32) for part in _split3(c * LOG2E))
    bias_lo = jnp.where(lane < n_heads, hi,
                        jnp.where(lane < 2 * n_heads, pltpu.roll(mid, n_heads, 1),
                                  jnp.where(lane < 3 * n_heads, pltpu.roll(lo, 2 * n_heads, 1), 0.0)))
    bias_hi = pltpu.roll(bias_lo, HEAD_DIM, 1)
    lower = lane < HEAD_DIM

    k = _head_rmsnorm(_dot(xn, wqkv_ref[:, d:2 * d]), kn_ref[...], seg_ref[...], exp_ref[...])
    k_ref[0] = k
    for h in range(n_heads):
        col = slice((h // 2) * LANES, (h // 2 + 1) * LANES)
        ka = jnp.where(lower, k[:, col], bias_hi) if h % 2 == 0 else jnp.where(lower, bias_lo, k[:, col])
        ka_ref[0, h] = ka.astype(BF16)

    q = _head_rmsnorm(_dot(xn, wqkv_ref[:, 0:d]), qn_ref[...], seg_ref[...], exp_ref[...])
    qs = q * (HEAD_DIM ** -0.5 * LOG2E)
    for h in range(n_heads):
        col = slice((h // 2) * LANES, (h // 2 + 1) * LANES)
        base = h + (HEAD_DIM if h % 2 == 0 else 0)
        pick = jnp.where(lane_row == base, -1.0, 0.0)
        pick = pick + jnp.where(lane_row == base + n_heads, -1.0, 0.0)
        pick = pick + jnp.where(lane_row == base + 2 * n_heads, -1.0, 0.0)
        qa = jnp.where(lower, qs[:, col], pick) if h % 2 == 0 else jnp.where(lower, pick, qs[:, col])
        qa_ref[0, h] = qa.astype(BF16)


def _fox_pre(x, weights, tm, n_heads):
    b, t, d = x.shape
    row = lambda i, j: (i, j, 0)
    head = lambda i, j: (i, 0, j, 0)
    arrays, specs = _split_layered(weights)
    return pl.pallas_call(
        _fox_pre_kernel,
        grid=(b, t // tm),
        in_specs=[pl.BlockSpec((1, tm, d), row)] + specs,
        out_specs=[
            pl.BlockSpec((1, n_heads, tm, LANES), head),
            pl.BlockSpec((1, n_heads, tm, LANES), head),
            pl.BlockSpec((1, n_heads, 1, VT_ROWS, tm), lambda i, j: (i, 0, j, 0, 0)),
            pl.BlockSpec((1, tm, d), row),
            pl.BlockSpec((1, tm, d), row),
            pl.BlockSpec((1, tm, n_heads), row),
        ],
        out_shape=[
            jax.ShapeDtypeStruct((b, n_heads, t, LANES), BF16),
            jax.ShapeDtypeStruct((b, n_heads, t, LANES), BF16),
            jax.ShapeDtypeStruct((b, n_heads, t // tm, VT_ROWS, tm), BF16),
            jax.ShapeDtypeStruct((b, t, d), F32),
            jax.ShapeDtypeStruct((b, t, d), F32),
            jax.ShapeDtypeStruct((b, t, n_heads), F32),
        ],
        scratch_shapes=[pltpu.VMEM((SUBLANES, LANES), F32)],
        compiler_params=_params("arbitrary", "arbitrary"),
        name="fox_pre",
    )(x, *arrays)


def _fox_attn_kernel(qa_ref, ka_ref, vt_ref, o_ref, acc_ref, st_a, st_b):
    pair, tq = qa_ref.shape[1], qa_ref.shape[2]
    vrows = vt_ref.shape[3]
    qi = pl.program_id(2)
    acc_ref[...] = jnp.zeros(acc_ref.shape, F32)

    def logits_into(st_ref, ki):
        start = pl.multiple_of(ki * tq, tq)
        for j in range(pair):
            st_ref[j] = _dot_nt(ka_ref[0, j, pl.ds(start, tq), :], qa_ref[0, j])

    def consume(st_ref, ki, m_prev, masked):
        m_out = []
        for j in range(pair):
            st = st_ref[j]
            if masked:
                key_idx = lax.broadcasted_iota(jnp.int32, (tq, tq), 0)
                qry_idx = lax.broadcasted_iota(jnp.int32, (tq, tq), 1)
                st = jnp.where(key_idx <= qry_idx, st, MASK_VALUE)
            m_new = jnp.maximum(m_prev[j], jnp.max(st, axis=0, keepdims=True))
            pt = jnp.exp2(st - m_new).astype(BF16)
            rows = slice(j * vrows, (j + 1) * vrows)
            acc_ref[rows, :] = jnp.exp2(m_prev[j] - m_new) * acc_ref[rows, :] + _dot(vt_ref[0, j, ki], pt)
            m_out.append(m_new)
        return tuple(m_out)

    odd = qi % 2
    m = tuple(jnp.full((1, tq), MASK_VALUE, F32) for _ in range(pair))

    def odd_start(m):
        logits_into(st_b, 0)
        logits_into(st_a, 1)
        return consume(st_b, 0, m, False)

    def even_start(m):
        logits_into(st_a, 0)
        return m

    m = lax.cond(odd == 1, odd_start, even_start, m)

    def two_blocks(i, m):
        k0 = odd + 2 * i
        logits_into(st_b, k0 + 1)
        m = consume(st_a, k0, m, False)
        logits_into(st_a, k0 + 2)
        return consume(st_b, k0 + 1, m, False)

    m = lax.fori_loop(0, qi // 2, two_blocks, m)
    consume(st_a, qi, m, True)
    outs = []
    for j in range(pair):
        base = j * vrows
        outs.append(acc_ref[base:base + HEAD_DIM, :] / acc_ref[base + HEAD_DIM:base + HEAD_DIM + 1, :])
    o_ref[0] = jnp.concatenate(outs, axis=0).T.astype(BF16)


def _fox_attn(qa, ka, vt):
    b, n_heads, t, _ = qa.shape
    nk, tq = vt.shape[2], vt.shape[4]
    pair = 2
    return pl.pallas_call(
        _fox_attn_kernel,
        grid=(b, n_heads // pair, t // tq),
        in_specs=[
            pl.BlockSpec((1, pair, tq, LANES), lambda i, h, j: (i, h, j, 0)),
            pl.BlockSpec((1, pair, t, LANES), lambda i, h, j: (i, h, 0, 0)),
            pl.BlockSpec((1, pair, nk, vt.shape[3], tq), lambda i, h, j: (i, h, 0, 0, 0)),
        ],
        out_specs=pl.BlockSpec((1, tq, pair * HEAD_DIM), lambda i, h, j: (i, j, h)),
        out_shape=jax.ShapeDtypeStruct((b, t, n_heads * HEAD_DIM), BF16),
        scratch_shapes=[
            pltpu.VMEM((pair * vt.shape[3], tq), F32),
            pltpu.VMEM((pair, tq, tq), F32),
            pltpu.VMEM((pair, tq, tq), F32),
        ],
        compiler_params=_params("arbitrary", "arbitrary", "arbitrary"),
        name="fox_attn",
    )(qa, ka, vt)


def _scan_rows(a, b, h0):
    rows, w = a.shape
    groups = rows // SUBLANES
    a3 = a.reshape(groups, SUBLANES, w)
    b3 = b.reshape(groups, SUBLANES, w)
    sub = lax.broadcasted_iota(jnp.int32, (groups, SUBLANES, w), 1)
    s = 1
    while s < SUBLANES:
        keep = sub >= s
        b3 = a3 * jnp.where(keep, pltpu.roll(b3, s, 1), 0.0) + b3
        a3 = a3 * jnp.where(keep, pltpu.roll(a3, s, 1), 1.0)
        s *= 2
    last = SUBLANES - 1
    entry = [h0]
    for g in range(groups - 1):
        entry.append(a3[g, last:last + 1, :] * entry[g] + b3[g, last:last + 1, :])
    return jnp.concatenate([a3[g] * entry[g] + b3[g] for g in range(groups)], axis=0)


def _lru_pre_kernel(x_ref, g_ref, win_ref, bin_ref, cw_ref, cb_ref, wai_ref, ba_ref, bi_ref,
                    lam_ref, z_ref, nh_ref, nc_ref, xbuf, hcarry):
    tm = x_ref.shape[1]
    w = cw_ref.shape[1]
    taps = cw_ref.shape[0]

    @pl.when(pl.program_id(1) == 0)
    def _():
        xbuf[0:SUBLANES, :] = jnp.zeros((SUBLANES, w), F32)
        hcarry[...] = jnp.zeros_like(hcarry)

    xn = _rmsnorm(x_ref[0], g_ref[...]).astype(BF16)
    xb = _dot(xn, win_ref[:, w:2 * w]) + bin_ref[:, w:2 * w]
    xbuf[SUBLANES:SUBLANES + tm, :] = xb
    y = cw_ref[taps - 1:taps, :] * xb + cb_ref[...]
    for k in range(1, taps):
        y = y + cw_ref[taps - 1 - k:taps - k, :] * xbuf[SUBLANES - k:SUBLANES - k + tm, :]
    nc_ref[0] = xbuf[SUBLANES + tm - (taps - 1):SUBLANES + tm, :]
    xbuf[0:SUBLANES, :] = xbuf[tm:tm + SUBLANES, :]

    a, bt = _lru_gates(y, wai_ref, ba_ref[...], bi_ref[...], lam_ref[...], w)
    h = _scan_rows(a, bt, hcarry[0:1, :])
    hcarry[0:1, :] = h[tm - 1:tm, :]
    nh_ref[0] = h[tm - 1:tm, :]
    gate = _gelu_tanh(_dot(xn, win_ref[:, 0:w]) + bin_ref[:, 0:w])
    z_ref[0] = (h * gate).astype(BF16)


def _lru_pre(x, weights, w, taps):
    b, t, d = x.shape
    tm = _row_tile(t, LRU_ROW_TILE)
    arrays, specs = _split_layered(weights)
    return pl.pallas_call(
        _lru_pre_kernel,
        grid=(b, t // tm),
        in_specs=[pl.BlockSpec((1, tm, d), lambda i, j: (i, j, 0))] + specs,
        out_specs=[
            pl.BlockSpec((1, tm, w), lambda i, j: (i, j, 0)),
            pl.BlockSpec((1, 1, w), lambda i, j: (i, 0, 0)),
            pl.BlockSpec((1, taps - 1, w), lambda i, j: (i, 0, 0)),
        ],
        out_shape=[
            jax.ShapeDtypeStruct((b, t, w), BF16),
            jax.ShapeDtypeStruct((b, 1, w), F32),
            jax.ShapeDtypeStruct((b, taps - 1, w), F32),
        ],
        scratch_shapes=[
            pltpu.VMEM((SUBLANES + tm, w), F32),
            pltpu.VMEM((SUBLANES, w), F32),
        ],
        compiler_params=_params("arbitrary", "arbitrary"),
        name="lru_pre",
    )(x, *arrays)


def _whole(shape):
    n = len(shape)
    return pl.BlockSpec(shape, lambda *_: (0,) * n)


def _single_step_call(kernel, inputs, out_shapes, name):
    inputs, specs = _split_layered(inputs)
    return pl.pallas_call(
        kernel,
        grid=(1,),
        in_specs=specs,
        out_specs=[_whole(s.shape) for s in out_shapes],
        out_shape=out_shapes,
        compiler_params=_params("arbitrary"),
        name=name,
    )(*inputs)


def _sconv_s_kernel(x_ref, g_ref, win_ref, cw_ref, b0_ref, b1_ref, z_ref, u_ref):
    d = x_ref.shape[1]
    xn = _rmsnorm(x_ref[...], g_ref[...]).astype(BF16)
    bg = _dot(xn, win_ref[:, 0:d])
    u = _dot(xn, win_ref[:, d:2 * d]) * _dot(xn, win_ref[:, 2 * d:3 * d])
    y = cw_ref[0:1, :] * b0_ref[...] + cw_ref[1:2, :] * b1_ref[...] + cw_ref[2:3, :] * u
    z_ref[...] = (bg * y).astype(BF16)
    u_ref[...] = u


def _post_s_kernel(x_ref, z_ref, wo_ref, g_ref, wg_ref, wu_ref, cw_ref, cb_ref, wd_ref,
                   b0_ref, b1_ref, o_ref, gn_ref):
    x1 = x_ref[...] + _dot(z_ref[...], wo_ref[...])
    xn = _rmsnorm(x1, g_ref[...]).astype(BF16)

    def gstore(sl, g):
        gn_ref[:, sl] = g

    def gprev(sl, k):
        return (b1_ref if k == 1 else b0_ref)[:, sl]

    o_ref[...] = _ffn_body(x1, xn, gprev, gstore, wg_ref, wu_ref, cw_ref, cb_ref, wd_ref)


def _fox_pre_s_kernel(x_ref, g_ref, wqkv_ref, wf_ref, bf_ref, qn_ref, kn_ref, seg_ref, exp_ref,
                      q_ref, k_ref, v_ref, lf_ref):
    xn = _rmsnorm(x_ref[...], g_ref[...]).astype(BF16)
    q, k, v, lf = _fox_project(xn, wqkv_ref, wf_ref, bf_ref, qn_ref, kn_ref, seg_ref, exp_ref)
    q_ref[...] = q * (HEAD_DIM ** -0.5)
    k_ref[...] = k
    v_ref[...] = v
    lf_ref[...] = lf


def _lru_s_kernel(x_ref, g_ref, win_ref, bin_ref, cw_ref, cb_ref, wai_ref, ba_ref, bi_ref, lam_ref,
                  h0_ref, c0_ref, c1_ref, c2_ref, z_ref, h_ref, xb_ref):
    w = cw_ref.shape[1]
    xn = _rmsnorm(x_ref[...], g_ref[...]).astype(BF16)
    gate = _gelu_tanh(_dot(xn, win_ref[:, 0:w]) + bin_ref[:, 0:w])
    xb = _dot(xn, win_ref[:, w:2 * w]) + bin_ref[:, w:2 * w]
    y = (cw_ref[0:1, :] * c0_ref[...] + cw_ref[1:2, :] * c1_ref[...] + cw_ref[2:3, :] * c2_ref[...]
         + cw_ref[3:4, :] * xb + cb_ref[...])
    a, bt = _lru_gates(y, wai_ref, ba_ref[...], bi_ref[...], lam_ref[...], w)
    h = a * h0_ref[...] + bt
    h_ref[...] = h
    xb_ref[...] = xb
    z_ref[...] = (h * gate).astype(BF16)


def _decode_attn_kernel(pt_ref, *refs, n_slots):
    k_refs = refs[:n_slots]
    v_refs = refs[n_slots:2 * n_slots]
    lf_refs = refs[2 * n_slots:3 * n_slots]
    (q_ref, kn_ref, qb_ref, vnb_ref, lfn_ref, later_ref,
     o_ref, m_ref, l_ref, acc_ref, lf_after) = refs[3 * n_slots:]
    n_heads = q_ref.shape[1]

    @pl.when(pl.program_id(1) == 0)
    def _():
        lf_after[...] = jnp.zeros_like(lf_after)
        m_ref[...] = jnp.sum(q_ref[0] * kn_ref[0], axis=1, keepdims=True)
        l_ref[...] = jnp.ones_like(l_ref)
        lane = lax.broadcasted_iota(jnp.int32, acc_ref.shape[1:], 1)
        for h in range(n_heads):
            acc_ref[h] = jnp.where(lane == 0, vnb_ref[0, h], 0.0)

    logits = [None] * n_slots
    for i in reversed(range(n_slots)):
        lf = lf_refs[i][...]
        bias = _dot_exact_rhs01(_split3(lf), later_ref[...]) + lf_after[...] + lfn_ref[0]
        lf_after[...] = lf_after[...] + jnp.sum(lf, axis=1, keepdims=True)
        rows = [jnp.sum(k_refs[i][h] * qb_ref[0, h], axis=0, keepdims=True) for h in range(n_heads)]
        logits[i] = jnp.concatenate(rows, axis=0) + bias
    m_prev = m_ref[...]
    m_new = m_prev
    for s in logits:
        m_new = jnp.maximum(m_new, jnp.max(s, axis=1, keepdims=True))
    alpha = jnp.exp(m_prev - m_new)
    probs = [jnp.exp(s - m_new) for s in logits]
    l_new = alpha * l_ref[...]
    for p in probs:
        l_new = l_new + jnp.sum(p, axis=1, keepdims=True)
    l_ref[...] = l_new
    m_ref[...] = m_new
    for h in range(n_heads):
        a = acc_ref[h] * alpha[h:h + 1, :]
        for i in range(n_slots):
            a = a + probs[i][h:h + 1, :] * v_refs[i][h]
        acc_ref[h] = a

    @pl.when(pl.program_id(1) == pl.num_programs(1) - 1)
    def _():
        for h in range(n_heads):
            o_ref[0, h] = jnp.sum(acc_ref[h], axis=1, keepdims=True) / l_ref[h:h + 1, :]


def _decode_attn(k_t, v_t, logf_t, layer, page_table, q3, kn3, qb, vnb, lf_new, later):
    bs, n_pages = page_table.shape
    n_heads, page = k_t.shape[2], k_t.shape[4]
    g = DECODE_PAGES_PER_STEP if n_pages % DECODE_PAGES_PER_STEP == 0 else 1
    n_groups = n_pages // g

    def page_spec(i):
        def idx(b, p, pt):
            return (layer, pt[b, (n_groups - 1 - p) * g + i], 0, 0, 0)
        return pl.BlockSpec((None, None, n_heads, HEAD_DIM, page), idx)

    def logf_spec(i):
        def idx(b, p, pt):
            return (layer, pt[b, (n_groups - 1 - p) * g + i], 0, 0)
        return pl.BlockSpec((None, None, n_heads, page), idx)

    per_seq = pl.BlockSpec((1, n_heads, HEAD_DIM), lambda b, p, pt: (b, 0, 0))
    per_seq_b = pl.BlockSpec((1, n_heads, HEAD_DIM, page), lambda b, p, pt: (b, 0, 0, 0))
    grid_spec = pltpu.PrefetchScalarGridSpec(
        num_scalar_prefetch=1,
        grid=(bs, n_groups),
        in_specs=([page_spec(i) for i in range(g)] + [page_spec(i) for i in range(g)]
                  + [logf_spec(i) for i in range(g)]
                  + [per_seq, per_seq, per_seq_b, per_seq_b,
                     pl.BlockSpec((1, n_heads, 1), lambda b, p, pt: (b, 0, 0)),
                     pl.BlockSpec((page, page), lambda b, p, pt: (0, 0))]),
        out_specs=pl.BlockSpec((1, n_heads, HEAD_DIM, 1), lambda b, p, pt: (b, 0, 0, 0)),
        scratch_shapes=[
            pltpu.VMEM((n_heads, 1), F32),
            pltpu.VMEM((n_heads, 1), F32),
            pltpu.VMEM((n_heads, HEAD_DIM, page), F32),
            pltpu.VMEM((n_heads, 1), F32),
        ],
    )
    return pl.pallas_call(
        functools.partial(_decode_attn_kernel, n_slots=g),
        grid_spec=grid_spec,
        out_shape=jax.ShapeDtypeStruct((bs, n_heads, HEAD_DIM, 1), F32),
        compiler_params=_params("arbitrary", "arbitrary"),
        name="decode_attn",
    )(page_table, *([k_t] * g), *([v_t] * g), *([logf_t] * g), q3, kn3, qb, vnb, lf_new, later)


def _block_diag(w):
    n, c, _ = w.shape
    eye = jnp.eye(n, dtype=w.dtype)
    return (eye[:, None, :, None] * w[:, :, None, :]).reshape(n * c, n * c)


def kernel(x_prompt, x_sample, state_sconv, cache_k, cache_v, cache_logf, page_table, state_lru_h, state_lru_conv, state_ffn_conv, mix_norm, ffn_norm, sc_w_in, sc_conv_w, sc_w_out, fox_w_qkv, fox_w_f, fox_b_f, fox_q_norm, fox_k_norm, fox_w_o, lru_w_in, lru_b_in, lru_conv_w, lru_conv_b, lru_w_a, lru_b_a, lru_w_i, lru_b_i, lru_lambda, lru_w_out, ffn_w_gate, ffn_w_up, ffn_conv_w, ffn_conv_b, ffn_w_down):
    depth, d = mix_norm.shape
    bp, tp, _ = x_prompt.shape
    bs, ts, _ = x_sample.shape
    assert ts == 1, "the sample group carries one new row per sequence"
    n_heads = fox_w_f.shape[-1]
    assert n_heads * HEAD_DIM == d and n_heads % 2 == 0 and 3 * n_heads <= HEAD_DIM
    n_mixers = 3
    f = ffn_w_gate.shape[-1]
    page = cache_k.shape[2]

    xp = x_prompt
    xs = x_sample.reshape(bs, d)
    row2 = lambda v: v.reshape(1, -1)

    lane_head = jnp.arange(d, dtype=jnp.int32) // HEAD_DIM
    seg = (lane_head[:, None] == jnp.arange(LANES, dtype=jnp.int32)[None, :]).astype(BF16)
    expand = seg.T
    tm_fox = _row_tile(tp, ROW_TILE)
    tri = (jnp.arange(tm_fox)[:, None] >= jnp.arange(tm_fox)[None, :]).astype(BF16)
    later = (jnp.arange(page)[:, None] > jnp.arange(page)[None, :]).astype(BF16)
    k_t = jnp.transpose(cache_k, (0, 1, 3, 4, 2))
    v_t = jnp.transpose(cache_v, (0, 1, 3, 4, 2))
    logf_t = jnp.transpose(cache_logf, (0, 1, 3, 2))

    bf = lambda w: w.astype(BF16)
    sc_w_in_b, sc_w_out_b = bf(sc_w_in), bf(sc_w_out)
    fox_w_qkv_b, fox_w_o_b = bf(fox_w_qkv), bf(fox_w_o)
    lru_w_in_b, lru_w_out_b = bf(lru_w_in), bf(lru_w_out)
    ffn_w_gate_b, ffn_w_up_b, ffn_w_down_b = bf(ffn_w_gate), bf(ffn_w_up), bf(ffn_w_down)

    sc_p, sc_s, fc_p, fc_s = [], [], [], []
    k_p, v_p, lf_p, k_s, v_s, lf_s = [], [], [], [], [], []
    lh_p, lh_s, lc_p, lc_s = [], [], [], []

    for i in range(depth):
        j = i // n_mixers
        g_mix = (mix_norm, i)
        if i % n_mixers == 0:
            weights = [g_mix, (sc_w_in_b, j), (sc_conv_w, j)]
            w_out = (sc_w_out_b, j)
            zp, nbp = _sconv_pre(xp, weights)
            zs, us = _single_step_call(
                _sconv_s_kernel,
                [xs, *weights, state_sconv[j, :, 0], state_sconv[j, :, 1]],
                [jax.ShapeDtypeStruct((bs, d), BF16), jax.ShapeDtypeStruct((bs, d), F32)],
                "sconv_sample")
            sc_p.append(nbp)
            sc_s.append(jnp.stack([state_sconv[j, :, 1], us], axis=1))
        elif i % n_mixers == 1:
            w_f = jnp.pad(fox_w_f[j], ((0, 0), (0, LANES - n_heads))).astype(BF16)
            b_f = jnp.pad(fox_b_f[j], (0, LANES - n_heads)).reshape(1, LANES)
            qn = row2(jnp.tile(fox_q_norm[j], n_heads))
            kn = row2(jnp.tile(fox_k_norm[j], n_heads))
            weights = [g_mix, (fox_w_qkv_b, j), w_f, b_f, qn, kn, seg, expand]
            w_out = (fox_w_o_b, j)
            qa, ka, vt, kp, vp, lfp = _fox_pre(xp, weights + [tri], tm_fox, n_heads)
            zp = _fox_attn(qa, ka, vt)
            k_p.append(kp.reshape(bp, tp, n_heads, HEAD_DIM))
            v_p.append(vp.reshape(bp, tp, n_heads, HEAD_DIM))
            lf_p.append(lfp)

            qs, ks, vs, lfs = _single_step_call(
                _fox_pre_s_kernel,
                [xs, *weights],
                [jax.ShapeDtypeStruct((bs, d), F32)] * 3 + [jax.ShapeDtypeStruct((bs, LANES), F32)],
                "fox_pre_sample")
            lfs = lfs[:, :n_heads]
            qs3 = qs.reshape(bs, n_heads, HEAD_DIM)
            ks3 = ks.reshape(bs, n_heads, HEAD_DIM)
            vs3 = vs.reshape(bs, n_heads, HEAD_DIM)
            along_lanes = lambda a: jnp.broadcast_to(a[..., None], (bs, n_heads, HEAD_DIM, page))
            os4 = _decode_attn(k_t, v_t, logf_t, j, page_table, qs3, ks3, along_lanes(qs3), along_lanes(vs3),
                               lfs.reshape(bs, n_heads, 1), later)
            zs = os4.reshape(bs, d).astype(BF16)
            k_s.append(ks3.reshape(bs, 1, n_heads, HEAD_DIM))
            v_s.append(vs3.reshape(bs, 1, n_heads, HEAD_DIM))
            lf_s.append(lfs.reshape(bs, 1, n_heads))
        else:
            w = lru_conv_w.shape[-1]
            taps = lru_conv_w.shape[1]
            nb = lru_w_a.shape[1]
            assert nb % 2 == 0 and (w // 2) % LANES == 0
            w_ai = jnp.stack([
                jnp.concatenate([_block_diag(lru_w_a[j, k * nb // 2:(k + 1) * nb // 2]),
                                 _block_diag(lru_w_i[j, k * nb // 2:(k + 1) * nb // 2])], axis=1)
                for k in range(2)]).astype(BF16)
            weights = [g_mix, (lru_w_in_b, j), (lru_b_in, j), (lru_conv_w, j), (lru_conv_b, j), w_ai,
                       (lru_b_a, j), (lru_b_i, j), (lru_lambda, j)]
            w_out = (lru_w_out_b, j)
            zp, nhp, ncp = _lru_pre(xp, weights, w, taps)
            zs, nhs, xbs = _single_step_call(
                _lru_s_kernel,
                [xs, *weights, state_lru_h[j],
                 state_lru_conv[j, :, 0], state_lru_conv[j, :, 1], state_lru_conv[j, :, 2]],
                [jax.ShapeDtypeStruct((bs, w), BF16), jax.ShapeDtypeStruct((bs, w), F32),
                 jax.ShapeDtypeStruct((bs, w), F32)],
                "lru_sample")
            lh_p.append(nhp.reshape(bp, w))
            lh_s.append(nhs)
            lc_p.append(ncp)
            lc_s.append(jnp.concatenate([state_lru_conv[j, :, 1:], xbs[:, None, :]], axis=1))

        post_w = [w_out, (ffn_norm, i), (ffn_w_gate_b, i), (ffn_w_up_b, i), (ffn_conv_w, i), (ffn_conv_b, i),
                  (ffn_w_down_b, i)]
        xp, nfp = _post(xp, zp, post_w, f)
        xs, gs = _single_step_call(
            _post_s_kernel,
            [xs, zs, *post_w, state_ffn_conv[i, :, 0], state_ffn_conv[i, :, 1]],
            [jax.ShapeDtypeStruct((bs, d), F32), jax.ShapeDtypeStruct((bs, f), F32)],
            "post_ffn_sample")
        fc_p.append(nfp)
        fc_s.append(jnp.stack([state_ffn_conv[i, :, 1], gs], axis=1))

    return (xp, xs.reshape(bs, 1, d),
            jnp.stack(sc_p), jnp.stack(sc_s),
            jnp.stack(k_p), jnp.stack(v_p), jnp.stack(lf_p),
            jnp.stack(k_s), jnp.stack(v_s), jnp.stack(lf_s),
            jnp.stack(lh_p), jnp.stack(lh_s),
            jnp.stack(lc_p), jnp.stack(lc_s),
            jnp.stack(fc_p), jnp.stack(fc_s))
```

```python
import functools

import jax
import jax.numpy as jnp
from jax import lax
from jax.experimental import pallas as pl
from jax.experimental.pallas import tpu as pltpu

F32 = jnp.float32
BF16 = jnp.bfloat16

EPS = 1e-6
HEAD_DIM = 64
LRU_C = 8.0
MASK_VALUE = -1e30
LOG2E = 1.4426950408889634
VT_ROWS = HEAD_DIM + 16

SUBLANES = 8
LANES = 128
V7X_VMEM_LIMIT_BYTES = 56 * 1024 * 1024
ROW_TILE = 512
LRU_ROW_TILE = 256
DECODE_PAGES_PER_STEP = 16


def _dot(a, b):
    return jnp.dot(a, b, preferred_element_type=F32)


def _dot_nt(a, b):
    return lax.dot_general(a, b, (((1,), (1,)), ((), ())), preferred_element_type=F32)


def _split2(x):
    hi = x.astype(BF16)
    lo = (x - hi.astype(F32)).astype(BF16)
    return hi, lo


def _split3(x):
    hi = x.astype(BF16)
    r = x - hi.astype(F32)
    mid = r.astype(BF16)
    lo = (r - mid.astype(F32)).astype(BF16)
    return hi, mid, lo


def _dot_exact_rhs01(x_parts, m01):
    out = _dot(x_parts[0], m01)
    for p in x_parts[1:]:
        out = out + _dot(p, m01)
    return out


def _rmsnorm(x, g):
    ms = jnp.mean(x * x, axis=-1, keepdims=True)
    return x * lax.rsqrt(ms + EPS) * g


def _sigmoid(x):
    return 1.0 / (1.0 + jnp.exp2(x * (-LOG2E)))


def _softplus(x):
    return jnp.maximum(x, 0.0) + jnp.log1p(jnp.exp(-jnp.abs(x)))


def _gelu_tanh(x):
    k = -2.0 * LOG2E * 0.7978845608028654
    return x / (1.0 + jnp.exp2(x * (k + (k * 0.044715) * (x * x))))


def _resident(shape):
    n = len(shape)
    return pl.BlockSpec(shape, lambda *_: (0,) * n, pipeline_mode=pl.Buffered(1))


def _layer_spec(stacked, layer):
    tail = stacked.shape[1:]
    zeros = (0,) * len(tail)
    return pl.BlockSpec((None,) + tail, lambda *_: (layer,) + zeros, pipeline_mode=pl.Buffered(1))


def _split_layered(operands):
    arrays, specs = [], []
    for op in operands:
        if isinstance(op, tuple):
            stacked, layer = op
            if stacked.ndim == 2:
                stacked = stacked[:, None, :]
            arrays.append(stacked)
            specs.append(_layer_spec(stacked, layer))
        else:
            arrays.append(op)
            specs.append(_resident(op.shape))
    return arrays, specs


def _params(*sem):
    return pltpu.CompilerParams(dimension_semantics=sem, vmem_limit_bytes=V7X_VMEM_LIMIT_BYTES)


def _row_tile(t, pref):
    return pref if t % pref == 0 else t


def _ffn_chunks(f):
    for n in (1,):
        if f % (n * LANES) == 0:
            return n
    return 1


def _head_rmsnorm(x, gain_tiled, seg, expand):
    ss = _dot_exact_rhs01(_split2(x * x), seg)
    inv = lax.rsqrt(ss * (1.0 / HEAD_DIM) + EPS)
    return x * _dot_exact_rhs01(_split2(inv), expand) * gain_tiled


def _log_sigmoid(x):
    return -_softplus(-x)


def _lru_gates(y, wai_ref, ba, bi, lam, w):
    yb = y.astype(BF16)
    half = w // 2
    parts = [_dot(yb[:, k * half:(k + 1) * half], wai_ref[k]) for k in range(2)]
    r = _sigmoid(jnp.concatenate([pt[:, 0:half] for pt in parts], axis=1) + ba)
    i = _sigmoid(jnp.concatenate([pt[:, half:2 * half] for pt in parts], axis=1) + bi)
    a = jnp.exp2(r * ((-LRU_C * LOG2E) * _softplus(-lam)))
    b = jnp.sqrt(1.0 - a * a) * (i * y)
    return a, b


def _sconv_pre_kernel(x_ref, g_ref, win_ref, cw_ref, z_ref, nb_ref, ubuf):
    tm, d = x_ref.shape[1], x_ref.shape[2]

    @pl.when(pl.program_id(1) == 0)
    def _():
        ubuf[0:SUBLANES, :] = jnp.zeros((SUBLANES, d), F32)

    xn = _rmsnorm(x_ref[0], g_ref[...]).astype(BF16)
    u = _dot(xn, win_ref[:, d:2 * d]) * _dot(xn, win_ref[:, 2 * d:3 * d])
    ubuf[SUBLANES:SUBLANES + tm, :] = u
    y = (cw_ref[0:1, :] * ubuf[SUBLANES - 2:SUBLANES - 2 + tm, :]
         + cw_ref[1:2, :] * ubuf[SUBLANES - 1:SUBLANES - 1 + tm, :]
         + cw_ref[2:3, :] * u)
    z_ref[0] = (_dot(xn, win_ref[:, 0:d]) * y).astype(BF16)
    nb_ref[0] = ubuf[SUBLANES + tm - 2:SUBLANES + tm, :]
    ubuf[0:SUBLANES, :] = ubuf[tm:tm + SUBLANES, :]


def _sconv_pre(x, weights):
    b, t, d = x.shape
    tm = _row_tile(t, ROW_TILE)
    arrays, specs = _split_layered(weights)
    return pl.pallas_call(
        _sconv_pre_kernel,
        grid=(b, t // tm),
        in_specs=[pl.BlockSpec((1, tm, d), lambda i, j: (i, j, 0))] + specs,
        out_specs=[
            pl.BlockSpec((1, tm, d), lambda i, j: (i, j, 0)),
            pl.BlockSpec((1, 2, d), lambda i, j: (i, 0, 0)),
        ],
        out_shape=[
            jax.ShapeDtypeStruct((b, t, d), BF16),
            jax.ShapeDtypeStruct((b, 2, d), F32),
        ],
        scratch_shapes=[pltpu.VMEM((SUBLANES + tm, d), F32)],
        compiler_params=_params("arbitrary", "arbitrary"),
        name="sconv_pre",
    )(x, *arrays)


def _ffn_body(x1, xn, gprev_fn, gstore_fn, wg_ref, wu_ref, cw_ref, cb_ref, wd_ref):
    f = wg_ref.shape[1]
    n_chunks = _ffn_chunks(f)
    fc = f // n_chunks
    acc = x1
    for c in range(n_chunks):
        sl = slice(c * fc, (c + 1) * fc)
        g = _dot(xn, wg_ref[:, sl])
        gstore_fn(sl, g)
        gc = (cw_ref[0:1, sl] * gprev_fn(sl, 2) + cw_ref[1:2, sl] * gprev_fn(sl, 1)
              + cw_ref[2:3, sl] * g + cb_ref[:, sl])
        up = _dot(xn, wu_ref[:, sl])
        hmid = (gc * _sigmoid(gc) * up).astype(BF16)
        acc = acc + _dot(hmid, wd_ref[sl, :])
    return acc


def _post_kernel(x_ref, z_ref, wo_ref, g_ref, wg_ref, wu_ref, cw_ref, cb_ref, wd_ref,
                 o_ref, nb_ref, gbuf):
    tm = x_ref.shape[1]
    f = wg_ref.shape[1]

    @pl.when(pl.program_id(1) == 0)
    def _():
        gbuf[0:SUBLANES, :] = jnp.zeros((SUBLANES, f), F32)

    x1 = x_ref[0] + _dot(z_ref[0], wo_ref[...])
    xn = _rmsnorm(x1, g_ref[...]).astype(BF16)

    def gstore(sl, g):
        gbuf[SUBLANES:SUBLANES + tm, sl] = g

    def gprev(sl, k):
        return gbuf[SUBLANES - k:SUBLANES - k + tm, sl]

    o_ref[0] = _ffn_body(x1, xn, gprev, gstore, wg_ref, wu_ref, cw_ref, cb_ref, wd_ref)
    nb_ref[0] = gbuf[SUBLANES + tm - 2:SUBLANES + tm, :]
    gbuf[0:SUBLANES, :] = gbuf[tm:tm + SUBLANES, :]


def _post(x, z, weights, f):
    b, t, d = x.shape
    kz = z.shape[2]
    tm = _row_tile(t, ROW_TILE)
    arrays, specs = _split_layered(weights)
    return pl.pallas_call(
        _post_kernel,
        grid=(b, t // tm),
        in_specs=[
            pl.BlockSpec((1, tm, d), lambda i, j: (i, j, 0)),
            pl.BlockSpec((1, tm, kz), lambda i, j: (i, j, 0)),
        ] + specs,
        out_specs=[
            pl.BlockSpec((1, tm, d), lambda i, j: (i, j, 0)),
            pl.BlockSpec((1, 2, f), lambda i, j: (i, 0, 0)),
        ],
        out_shape=[
            jax.ShapeDtypeStruct((b, t, d), F32),
            jax.ShapeDtypeStruct((b, 2, f), F32),
        ],
        scratch_shapes=[pltpu.VMEM((SUBLANES + tm, f), F32)],
        compiler_params=_params("arbitrary", "arbitrary"),
        name="post_ffn",
    )(x, z, *arrays)


def _fox_project(xn, wqkv_ref, wf_ref, bf_ref, qn_ref, kn_ref, seg_ref, exp_ref):
    d = xn.shape[1]
    q = _head_rmsnorm(_dot(xn, wqkv_ref[:, 0:d]), qn_ref[...], seg_ref[...], exp_ref[...])
    k = _head_rmsnorm(_dot(xn, wqkv_ref[:, d:2 * d]), kn_ref[...], seg_ref[...], exp_ref[...])
    v = _dot(xn, wqkv_ref[:, 2 * d:3 * d])
    lf = _log_sigmoid(_dot(xn, wf_ref[...]) + bf_ref[...])
    return q, k, v, lf


def _dot_exact_rhs01_lhs(m01, x_parts):
    out = _dot(m01, x_parts[0])
    for p in x_parts[1:]:
        out = out + _dot(m01, p)
    return out


def _fox_pre_kernel(x_ref, g_ref, wqkv_ref, wf_ref, bf_ref, qn_ref, kn_ref, seg_ref, exp_ref,
                    tri_ref, qa_ref, ka_ref, vt_ref, k_ref, v_ref, lf_ref, carry):
    tm = x_ref.shape[1]
    n_heads = lf_ref.shape[2]

    @pl.when(pl.program_id(1) == 0)
    def _():
        carry[...] = jnp.zeros_like(carry)

    xn = _rmsnorm(x_ref[0], g_ref[...]).astype(BF16)
    d = xn.shape[1]
    v = _dot(xn, wqkv_ref[:, 2 * d:3 * d])
    v_ref[0] = v
    vt_ref[0, :, 0, 0:HEAD_DIM, :] = v.T.reshape(n_heads, HEAD_DIM, tm).astype(BF16)
    pad_rows = vt_ref.shape[3] - HEAD_DIM
    first = lax.broadcasted_iota(jnp.int32, (n_heads, pad_rows, tm), 1) == 0
    vt_ref[0, :, 0, HEAD_DIM:HEAD_DIM + pad_rows, :] = jnp.where(first, 1.0, 0.0).astype(BF16)

    lf = _log_sigmoid(_dot(xn, wf_ref[...]) + bf_ref[...])
    lf_ref[0] = lf[:, 0:n_heads]
    c = _dot_exact_rhs01_lhs(tri_ref[...], _split3(lf)) + carry[0:1, :]
    carry[0:1, :] = c[tm - 1:tm, :]
    lane = lax.broadcasted_iota(jnp.int32, (tm, LANES), 1)
    lane_row = lax.broadcasted_iota(jnp.int32, (1, LANES), 1)
    hi, mid, lo = (part.astype(F32) for part in _split3(c * LOG2E))
    bias_lo = jnp.where(lane < n_heads, hi,
                        jnp.where(lane < 2 * n_heads, pltpu.roll(mid, n_heads, 1),
                                  jnp.where(lane < 3 * n_heads, pltpu.roll(lo, 2 * n_heads, 1), 0.0)))
    bias_hi = pltpu.roll(bias_lo, HEAD_DIM, 1)
    lower = lane < HEAD_DIM

    k = _head_rmsnorm(_dot(xn, wqkv_ref[:, d:2 * d]), kn_ref[...], seg_ref[...], exp_ref[...])
    k_ref[0] = k
    for h in range(n_heads):
        col = slice((h // 2) * LANES, (h // 2 + 1) * LANES)
        ka = jnp.where(lower, k[:, col], bias_hi) if h % 2 == 0 else jnp.where(lower, bias_lo, k[:, col])
        ka_ref[0, h] = ka.astype(BF16)

    q = _head_rmsnorm(_dot(xn, wqkv_ref[:, 0:d]), qn_ref[...], seg_ref[...], exp_ref[...])
    qs = q * (HEAD_DIM ** -0.5 * LOG2E)
    for h in range(n_heads):
        col = slice((h // 2) * LANES, (h // 2 + 1) * LANES)
        base = h + (HEAD_DIM if h % 2 == 0 else 0)
        pick = jnp.where(lane_row == base, -1.0, 0.0)
        pick = pick + jnp.where(lane_row == base + n_heads, -1.0, 0.0)
        pick = pick + jnp.where(lane_row == base + 2 * n_heads, -1.0, 0.0)
        qa = jnp.where(lower, qs[:, col], pick) if h % 2 == 0 else jnp.where(lower, pick, qs[:, col])
        qa_ref[0, h] = qa.astype(BF16)


def _fox_pre(x, weights, tm, n_heads):
    b, t, d = x.shape
    row = lambda i, j: (i, j, 0)
    head = lambda i, j: (i, 0, j, 0)
    arrays, specs = _split_layered(weights)
    return pl.pallas_call(
        _fox_pre_kernel,
        grid=(b, t // tm),
        in_specs=[pl.BlockSpec((1, tm, d), row)] + specs,
        out_specs=[
            pl.BlockSpec((1, n_heads, tm, LANES), head),
            pl.BlockSpec((1, n_heads, tm, LANES), head),
            pl.BlockSpec((1, n_heads, 1, VT_ROWS, tm), lambda i, j: (i, 0, j, 0, 0)),
            pl.BlockSpec((1, tm, d), row),
            pl.BlockSpec((1, tm, d), row),
            pl.BlockSpec((1, tm, n_heads), row),
        ],
        out_shape=[
            jax.ShapeDtypeStruct((b, n_heads, t, LANES), BF16),
            jax.ShapeDtypeStruct((b, n_heads, t, LANES), BF16),
            jax.ShapeDtypeStruct((b, n_heads, t // tm, VT_ROWS, tm), BF16),
            jax.ShapeDtypeStruct((b, t, d), F32),
            jax.ShapeDtypeStruct((b, t, d), F32),
            jax.ShapeDtypeStruct((b, t, n_heads), F32),
        ],
        scratch_shapes=[pltpu.VMEM((SUBLANES, LANES), F32)],
        compiler_params=_params("arbitrary", "arbitrary"),
        name="fox_pre",
    )(x, *arrays)


def _fox_attn_kernel(qa_ref, ka_ref, vt_ref, o_ref, acc_ref, st_a, st_b):
    pair, tq = qa_ref.shape[1], qa_ref.shape[2]
    vrows = vt_ref.shape[3]
    qi = pl.program_id(2)
    acc_ref[...] = jnp.zeros(acc_ref.shape, F32)

    def logits_into(st_ref, ki):
        start = pl.multiple_of(ki * tq, tq)
        for j in range(pair):
            st_ref[j] = _dot_nt(ka_ref[0, j, pl.ds(start, tq), :], qa_ref[0, j])

    def consume(st_ref, ki, m_prev, masked):
        m_out = []
        for j in range(pair):
            st = st_ref[j]
            if masked:
                key_idx = lax.broadcasted_iota(jnp.int32, (tq, tq), 0)
                qry_idx = lax.broadcasted_iota(jnp.int32, (tq, tq), 1)
                st = jnp.where(key_idx <= qry_idx, st, MASK_VALUE)
            m_new = jnp.maximum(m_prev[j], jnp.max(st, axis=0, keepdims=True))
            pt = jnp.exp2(st - m_new).astype(BF16)
            rows = slice(j * vrows, (j + 1) * vrows)
            acc_ref[rows, :] = jnp.exp2(m_prev[j] - m_new) * acc_ref[rows, :] + _dot(vt_ref[0, j, ki], pt)
            m_out.append(m_new)
        return tuple(m_out)

    odd = qi % 2
    m = tuple(jnp.full((1, tq), MASK_VALUE, F32) for _ in range(pair))

    def odd_start(m):
        logits_into(st_b, 0)
        logits_into(st_a, 1)
        return consume(st_b, 0, m, False)

    def even_start(m):
        logits_into(st_a, 0)
        return m

    m = lax.cond(odd == 1, odd_start, even_start, m)

    def two_blocks(i, m):
        k0 = odd + 2 * i
        logits_into(st_b, k0 + 1)
        m = consume(st_a, k0, m, False)
        logits_into(st_a, k0 + 2)
        return consume(st_b, k0 + 1, m, False)

    pairs = qi // 2
    lead = pairs % 2
    m = lax.cond(lead == 1, lambda m: two_blocks(0, m), lambda m: m, m)
    m = lax.fori_loop(0, pairs // 2,
                      lambda i, m: two_blocks(lead + 2 * i + 1, two_blocks(lead + 2 * i, m)), m)
    consume(st_a, qi, m, True)
    outs = []
    for j in range(pair):
        base = j * vrows
        outs.append(acc_ref[base:base + HEAD_DIM, :] / acc_ref[base + HEAD_DIM:base + HEAD_DIM + 1, :])
    o_ref[0] = jnp.concatenate(outs, axis=0).T.astype(BF16)


def _fox_attn(qa, ka, vt):
    b, n_heads, t, _ = qa.shape
    nk, tq = vt.shape[2], vt.shape[4]
    pair = 2
    return pl.pallas_call(
        _fox_attn_kernel,
        grid=(b, n_heads // pair, t // tq),
        in_specs=[
            pl.BlockSpec((1, pair, tq, LANES), lambda i, h, j: (i, h, j, 0)),
            pl.BlockSpec((1, pair, t, LANES), lambda i, h, j: (i, h, 0, 0)),
            pl.BlockSpec((1, pair, nk, vt.shape[3], tq), lambda i, h, j: (i, h, 0, 0, 0)),
        ],
        out_specs=pl.BlockSpec((1, tq, pair * HEAD_DIM), lambda i, h, j: (i, j, h)),
        out_shape=jax.ShapeDtypeStruct((b, t, n_heads * HEAD_DIM), BF16),
        scratch_shapes=[
            pltpu.VMEM((pair * vt.shape[3], tq), F32),
            pltpu.VMEM((pair, tq, tq), F32),
            pltpu.VMEM((pair, tq, tq), F32),
        ],
        compiler_params=_params("arbitrary", "arbitrary", "arbitrary"),
        name="fox_attn",
    )(qa, ka, vt)


def _scan_rows(a, b, h0):
    rows, w = a.shape
    groups = rows // SUBLANES
    a3 = a.reshape(groups, SUBLANES, w)
    b3 = b.reshape(groups, SUBLANES, w)
    sub = lax.broadcasted_iota(jnp.int32, (groups, SUBLANES, w), 1)
    s = 1
    while s < SUBLANES:
        keep = sub >= s
        b3 = a3 * jnp.where(keep, pltpu.roll(b3, s, 1), 0.0) + b3
        a3 = a3 * jnp.where(keep, pltpu.roll(a3, s, 1), 1.0)
        s *= 2
    last = SUBLANES - 1
    entry = [h0]
    for g in range(groups - 1):
        entry.append(a3[g, last:last + 1, :] * entry[g] + b3[g, last:last + 1, :])
    return jnp.concatenate([a3[g] * entry[g] + b3[g] for g in range(groups)], axis=0)


def _lru_pre_kernel(x_ref, g_ref, win_ref, bin_ref, cw_ref, cb_ref, wai_ref, ba_ref, bi_ref,
                    lam_ref, z_ref, nh_ref, nc_ref, xbuf, hcarry):
    tm = x_ref.shape[1]
    w = cw_ref.shape[1]
    taps = cw_ref.shape[0]

    @pl.when(pl.program_id(1) == 0)
    def _():
        xbuf[0:SUBLANES, :] = jnp.zeros((SUBLANES, w), F32)
        hcarry[...] = jnp.zeros_like(hcarry)

    xn = _rmsnorm(x_ref[0], g_ref[...]).astype(BF16)
    xb = _dot(xn, win_ref[:, w:2 * w]) + bin_ref[:, w:2 * w]
    xbuf[SUBLANES:SUBLANES + tm, :] = xb
    y = cw_ref[taps - 1:taps, :] * xb + cb_ref[...]
    for k in range(1, taps):
        y = y + cw_ref[taps - 1 - k:taps - k, :] * xbuf[SUBLANES - k:SUBLANES - k + tm, :]
    nc_ref[0] = xbuf[SUBLANES + tm - (taps - 1):SUBLANES + tm, :]
    xbuf[0:SUBLANES, :] = xbuf[tm:tm + SUBLANES, :]

    a, bt = _lru_gates(y, wai_ref, ba_ref[...], bi_ref[...], lam_ref[...], w)
    h = _scan_rows(a, bt, hcarry[0:1, :])
    hcarry[0:1, :] = h[tm - 1:tm, :]
    nh_ref[0] = h[tm - 1:tm, :]
    gate = _gelu_tanh(_dot(xn, win_ref[:, 0:w]) + bin_ref[:, 0:w])
    z_ref[0] = (h * gate).astype(BF16)


def _lru_pre(x, weights, w, taps):
    b, t, d = x.shape
    tm = _row_tile(t, LRU_ROW_TILE)
    arrays, specs = _split_layered(weights)
    return pl.pallas_call(
        _lru_pre_kernel,
        grid=(b, t // tm),
        in_specs=[pl.BlockSpec((1, tm, d), lambda i, j: (i, j, 0))] + specs,
        out_specs=[
            pl.BlockSpec((1, tm, w), lambda i, j: (i, j, 0)),
            pl.BlockSpec((1, 1, w), lambda i, j: (i, 0, 0)),
            pl.BlockSpec((1, taps - 1, w), lambda i, j: (i, 0, 0)),
        ],
        out_shape=[
            jax.ShapeDtypeStruct((b, t, w), BF16),
            jax.ShapeDtypeStruct((b, 1, w), F32),
            jax.ShapeDtypeStruct((b, taps - 1, w), F32),
        ],
        scratch_shapes=[
            pltpu.VMEM((SUBLANES + tm, w), F32),
            pltpu.VMEM((SUBLANES, w), F32),
        ],
        compiler_params=_params("arbitrary", "arbitrary"),
        name="lru_pre",
    )(x, *arrays)


def _whole(shape):
    n = len(shape)
    return pl.BlockSpec(shape, lambda *_: (0,) * n)


def _single_step_call(kernel, inputs, out_shapes, name):
    inputs, specs = _split_layered(inputs)
    return pl.pallas_call(
        kernel,
        grid=(1,),
        in_specs=specs,
        out_specs=[_whole(s.shape) for s in out_shapes],
        out_shape=out_shapes,
        compiler_params=_params("arbitrary"),
        name=name,
    )(*inputs)


def _sconv_s_kernel(x_ref, g_ref, win_ref, cw_ref, st_ref, z_ref, ns_ref):
    d = x_ref.shape[1]
    xn = _rmsnorm(x_ref[...], g_ref[...]).astype(BF16)
    bg = _dot(xn, win_ref[:, 0:d])
    u = _dot(xn, win_ref[:, d:2 * d]) * _dot(xn, win_ref[:, 2 * d:3 * d])
    b0, b1 = st_ref[:, 0, :], st_ref[:, 1, :]
    y = cw_ref[0:1, :] * b0 + cw_ref[1:2, :] * b1 + cw_ref[2:3, :] * u
    z_ref[...] = (bg * y).astype(BF16)
    ns_ref[:, 0, :] = b1
    ns_ref[:, 1, :] = u


def _post_s_kernel(x_ref, z_ref, wo_ref, g_ref, wg_ref, wu_ref, cw_ref, cb_ref, wd_ref,
                   st_ref, o_ref, ns_ref):
    x1 = x_ref[...] + _dot(z_ref[...], wo_ref[...])
    xn = _rmsnorm(x1, g_ref[...]).astype(BF16)
    ns_ref[:, 0, :] = st_ref[:, 1, :]

    def gstore(sl, g):
        ns_ref[:, 1, sl] = g

    def gprev(sl, k):
        return st_ref[:, 2 - k, sl]

    o_ref[...] = _ffn_body(x1, xn, gprev, gstore, wg_ref, wu_ref, cw_ref, cb_ref, wd_ref)


def _fox_pre_s_kernel(x_ref, g_ref, wqkv_ref, wf_ref, bf_ref, qn_ref, kn_ref, seg_ref, exp_ref,
                      q_ref, k_ref, v_ref, lf_ref):
    xn = _rmsnorm(x_ref[...], g_ref[...]).astype(BF16)
    q, k, v, lf = _fox_project(xn, wqkv_ref, wf_ref, bf_ref, qn_ref, kn_ref, seg_ref, exp_ref)
    q_ref[...] = q * (HEAD_DIM ** -0.5)
    k_ref[...] = k
    v_ref[...] = v
    lf_ref[...] = lf


def _lru_s_kernel(x_ref, g_ref, win_ref, bin_ref, cw_ref, cb_ref, wai_ref, ba_ref, bi_ref, lam_ref,
                  h0_ref, st_ref, z_ref, h_ref, ns_ref):
    w = cw_ref.shape[1]
    taps = cw_ref.shape[0]
    xn = _rmsnorm(x_ref[...], g_ref[...]).astype(BF16)
    gate = _gelu_tanh(_dot(xn, win_ref[:, 0:w]) + bin_ref[:, 0:w])
    xb = _dot(xn, win_ref[:, w:2 * w]) + bin_ref[:, w:2 * w]
    y = cw_ref[taps - 1:taps, :] * xb + cb_ref[...]
    for k in range(taps - 1):
        prev = st_ref[:, k, :]
        y = y + cw_ref[k:k + 1, :] * prev
        if k > 0:
            ns_ref[:, k - 1, :] = prev
    ns_ref[:, taps - 2, :] = xb
    a, bt = _lru_gates(y, wai_ref, ba_ref[...], bi_ref[...], lam_ref[...], w)
    h = a * h0_ref[...] + bt
    h_ref[...] = h
    z_ref[...] = (h * gate).astype(BF16)


def _decode_attn_kernel(pt_ref, *refs, n_slots):
    k_refs = refs[:n_slots]
    v_refs = refs[n_slots:2 * n_slots]
    lf_refs = refs[2 * n_slots:3 * n_slots]
    (q_ref, kn_ref, qb_ref, vnb_ref, lfn_ref, later_ref,
     o_ref, m_ref, l_ref, acc_ref, lf_after) = refs[3 * n_slots:]
    n_heads = q_ref.shape[1]

    @pl.when(pl.program_id(1) == 0)
    def _():
        lf_after[...] = jnp.zeros_like(lf_after)
        m_ref[...] = jnp.sum(q_ref[0] * kn_ref[0], axis=1, keepdims=True)
        l_ref[...] = jnp.ones_like(l_ref)
        lane = lax.broadcasted_iota(jnp.int32, acc_ref.shape[1:], 1)
        for h in range(n_heads):
            acc_ref[h] = jnp.where(lane == 0, vnb_ref[0, h], 0.0)

    logits = [None] * n_slots
    for i in reversed(range(n_slots)):
        lf = lf_refs[i][...]
        bias = _dot_exact_rhs01(_split3(lf), later_ref[...]) + lf_after[...] + lfn_ref[0]
        lf_after[...] = lf_after[...] + jnp.sum(lf, axis=1, keepdims=True)
        rows = [jnp.sum(k_refs[i][h] * qb_ref[0, h], axis=0, keepdims=True) for h in range(n_heads)]
        logits[i] = jnp.concatenate(rows, axis=0) + bias
    m_prev = m_ref[...]
    m_new = m_prev
    for s in logits:
        m_new = jnp.maximum(m_new, jnp.max(s, axis=1, keepdims=True))
    alpha = jnp.exp(m_prev - m_new)
    probs = [jnp.exp(s - m_new) for s in logits]
    l_new = alpha * l_ref[...]
    for p in probs:
        l_new = l_new + jnp.sum(p, axis=1, keepdims=True)
    l_ref[...] = l_new
    m_ref[...] = m_new
    for h in range(n_heads):
        a = acc_ref[h] * alpha[h:h + 1, :]
        for i in range(n_slots):
            a = a + probs[i][h:h + 1, :] * v_refs[i][h]
        acc_ref[h] = a

    @pl.when(pl.program_id(1) == pl.num_programs(1) - 1)
    def _():
        for h in range(n_heads):
            o_ref[0, h] = jnp.sum(acc_ref[h], axis=1, keepdims=True) / l_ref[h:h + 1, :]


def _decode_attn(k_t, v_t, logf_t, layer, page_table, q3, kn3, qb, vnb, lf_new, later):
    bs, n_pages = page_table.shape
    n_heads, page = k_t.shape[2], k_t.shape[4]
    g = DECODE_PAGES_PER_STEP if n_pages % DECODE_PAGES_PER_STEP == 0 else 1
    n_groups = n_pages // g

    def page_spec(i):
        def idx(b, p, pt):
            return (layer, pt[b, (n_groups - 1 - p) * g + i], 0, 0, 0)
        return pl.BlockSpec((None, None, n_heads, HEAD_DIM, page), idx)

    def logf_spec(i):
        def idx(b, p, pt):
            return (layer, pt[b, (n_groups - 1 - p) * g + i], 0, 0)
        return pl.BlockSpec((None, None, n_heads, page), idx)

    per_seq = pl.BlockSpec((1, n_heads, HEAD_DIM), lambda b, p, pt: (b, 0, 0))
    per_seq_b = pl.BlockSpec((1, n_heads, HEAD_DIM, page), lambda b, p, pt: (b, 0, 0, 0))
    grid_spec = pltpu.PrefetchScalarGridSpec(
        num_scalar_prefetch=1,
        grid=(bs, n_groups),
        in_specs=([page_spec(i) for i in range(g)] + [page_spec(i) for i in range(g)]
                  + [logf_spec(i) for i in range(g)]
                  + [per_seq, per_seq, per_seq_b, per_seq_b,
                     pl.BlockSpec((1, n_heads, 1), lambda b, p, pt: (b, 0, 0)),
                     pl.BlockSpec((page, page), lambda b, p, pt: (0, 0))]),
        out_specs=pl.BlockSpec((1, n_heads, HEAD_DIM, 1), lambda b, p, pt: (b, 0, 0, 0)),
        scratch_shapes=[
            pltpu.VMEM((n_heads, 1), F32),
            pltpu.VMEM((n_heads, 1), F32),
            pltpu.VMEM((n_heads, HEAD_DIM, page), F32),
            pltpu.VMEM((n_heads, 1), F32),
        ],
    )
    return pl.pallas_call(
        functools.partial(_decode_attn_kernel, n_slots=g),
        grid_spec=grid_spec,
        out_shape=jax.ShapeDtypeStruct((bs, n_heads, HEAD_DIM, 1), F32),
        compiler_params=_params("arbitrary", "arbitrary"),
        name="decode_attn",
    )(page_table, *([k_t] * g), *([v_t] * g), *([logf_t] * g), q3, kn3, qb, vnb, lf_new, later)


def _block_diag(w):
    n, c, _ = w.shape
    eye = jnp.eye(n, dtype=w.dtype)
    return (eye[:, None, :, None] * w[:, :, None, :]).reshape(n * c, n * c)


def kernel(x_prompt, x_sample, state_sconv, cache_k, cache_v, cache_logf, page_table, state_lru_h, state_lru_conv, state_ffn_conv, mix_norm, ffn_norm, sc_w_in, sc_conv_w, sc_w_out, fox_w_qkv, fox_w_f, fox_b_f, fox_q_norm, fox_k_norm, fox_w_o, lru_w_in, lru_b_in, lru_conv_w, lru_conv_b, lru_w_a, lru_b_a, lru_w_i, lru_b_i, lru_lambda, lru_w_out, ffn_w_gate, ffn_w_up, ffn_conv_w, ffn_conv_b, ffn_w_down):
    depth, d = mix_norm.shape
    bp, tp, _ = x_prompt.shape
    bs, ts, _ = x_sample.shape
    assert ts == 1, "the sample group carries one new row per sequence"
    n_heads = fox_w_f.shape[-1]
    assert n_heads * HEAD_DIM == d and n_heads % 2 == 0 and 3 * n_heads <= HEAD_DIM
    n_mixers = 3
    f = ffn_w_gate.shape[-1]
    page = cache_k.shape[2]

    xp = x_prompt
    xs = x_sample.reshape(bs, d)
    row2 = lambda v: v.reshape(1, -1)

    lane_head = jnp.arange(d, dtype=jnp.int32) // HEAD_DIM
    seg = (lane_head[:, None] == jnp.arange(LANES, dtype=jnp.int32)[None, :]).astype(BF16)
    expand = seg.T
    tm_fox = _row_tile(tp, ROW_TILE)
    tri = (jnp.arange(tm_fox)[:, None] >= jnp.arange(tm_fox)[None, :]).astype(BF16)
    later = (jnp.arange(page)[:, None] > jnp.arange(page)[None, :]).astype(BF16)
    k_t = jnp.transpose(cache_k, (0, 1, 3, 4, 2))
    v_t = jnp.transpose(cache_v, (0, 1, 3, 4, 2))
    logf_t = jnp.transpose(cache_logf, (0, 1, 3, 2))

    bf = lambda w: w.astype(BF16)
    sc_w_in_b, sc_w_out_b = bf(sc_w_in), bf(sc_w_out)
    fox_w_qkv_b, fox_w_o_b = bf(fox_w_qkv), bf(fox_w_o)
    lru_w_in_b, lru_w_out_b = bf(lru_w_in), bf(lru_w_out)
    ffn_w_gate_b, ffn_w_up_b, ffn_w_down_b = bf(ffn_w_gate), bf(ffn_w_up), bf(ffn_w_down)

    sc_p, sc_s, fc_p, fc_s = [], [], [], []
    k_p, v_p, lf_p, k_s, v_s, lf_s = [], [], [], [], [], []
    lh_p, lh_s, lc_p, lc_s = [], [], [], []

    for i in range(depth):
        j = i // n_mixers
        g_mix = (mix_norm, i)
        if i % n_mixers == 0:
            weights = [g_mix, (sc_w_in_b, j), (sc_conv_w, j)]
            w_out = (sc_w_out_b, j)
            zp, nbp = _sconv_pre(xp, weights)
            zs, us = _single_step_call(
                _sconv_s_kernel,
                [xs, *weights, (state_sconv, j)],
                [jax.ShapeDtypeStruct((bs, d), BF16), jax.ShapeDtypeStruct(state_sconv.shape[1:], F32)],
                "sconv_sample")
            sc_p.append(nbp)
            sc_s.append(us)
        elif i % n_mixers == 1:
            w_f = jnp.pad(fox_w_f[j], ((0, 0), (0, LANES - n_heads))).astype(BF16)
            b_f = jnp.pad(fox_b_f[j], (0, LANES - n_heads)).reshape(1, LANES)
            qn = row2(jnp.tile(fox_q_norm[j], n_heads))
            kn = row2(jnp.tile(fox_k_norm[j], n_heads))
            weights = [g_mix, (fox_w_qkv_b, j), w_f, b_f, qn, kn, seg, expand]
            w_out = (fox_w_o_b, j)
            qa, ka, vt, kp, vp, lfp = _fox_pre(xp, weights + [tri], tm_fox, n_heads)
            zp = _fox_attn(qa, ka, vt)
            k_p.append(kp.reshape(bp, tp, n_heads, HEAD_DIM))
            v_p.append(vp.reshape(bp, tp, n_heads, HEAD_DIM))
            lf_p.append(lfp)

            qs, ks, vs, lfs = _single_step_call(
                _fox_pre_s_kernel,
                [xs, *weights],
                [jax.ShapeDtypeStruct((bs, d), F32)] * 3 + [jax.ShapeDtypeStruct((bs, LANES), F32)],
                "fox_pre_sample")
            lfs = lfs[:, :n_heads]
            qs3 = qs.reshape(bs, n_heads, HEAD_DIM)
            ks3 = ks.reshape(bs, n_heads, HEAD_DIM)
            vs3 = vs.reshape(bs, n_heads, HEAD_DIM)
            along_lanes = lambda a: jnp.broadcast_to(a[..., None], (bs, n_heads, HEAD_DIM, page))
            os4 = _decode_attn(k_t, v_t, logf_t, j, page_table, qs3, ks3, along_lanes(qs3), along_lanes(vs3),
                               lfs.reshape(bs, n_heads, 1), later)
            zs = os4.reshape(bs, d).astype(BF16)
            k_s.append(ks3.reshape(bs, 1, n_heads, HEAD_DIM))
            v_s.append(vs3.reshape(bs, 1, n_heads, HEAD_DIM))
            lf_s.append(lfs.reshape(bs, 1, n_heads))
        else:
            w = lru_conv_w.shape[-1]
            taps = lru_conv_w.shape[1]
            nb = lru_w_a.shape[1]
            assert nb % 2 == 0 and (w // 2) % LANES == 0
            w_ai = jnp.stack([
                jnp.concatenate([_block_diag(lru_w_a[j, k * nb // 2:(k + 1) * nb // 2]),
                                 _block_diag(lru_w_i[j, k * nb // 2:(k + 1) * nb // 2])], axis=1)
                for k in range(2)]).astype(BF16)
            weights = [g_mix, (lru_w_in_b, j), (lru_b_in, j), (lru_conv_w, j), (lru_conv_b, j), w_ai,
                       (lru_b_a, j), (lru_b_i, j), (lru_lambda, j)]
            w_out = (lru_w_out_b, j)
            zp, nhp, ncp = _lru_pre(xp, weights, w, taps)
            zs, nhs, xbs = _single_step_call(
                _lru_s_kernel,
                [xs, *weights, (state_lru_h, j), (state_lru_conv, j)],
                [jax.ShapeDtypeStruct((bs, w), BF16), jax.ShapeDtypeStruct((bs, w), F32),
                 jax.ShapeDtypeStruct(state_lru_conv.shape[1:], F32)],
                "lru_sample")
            lh_p.append(nhp.reshape(bp, w))
            lh_s.append(nhs)
            lc_p.append(ncp)
            lc_s.append(xbs)

        post_w = [w_out, (ffn_norm, i), (ffn_w_gate_b, i), (ffn_w_up_b, i), (ffn_conv_w, i), (ffn_conv_b, i),
                  (ffn_w_down_b, i)]
        xp, nfp = _post(xp, zp, post_w, f)
        xs, gs = _single_step_call(
            _post_s_kernel,
            [xs, zs, *post_w, (state_ffn_conv, i)],
            [jax.ShapeDtypeStruct((bs, d), F32), jax.ShapeDtypeStruct(state_ffn_conv.shape[1:], F32)],
            "post_ffn_sample")
        fc_p.append(nfp)
        fc_s.append(gs)

    return (xp, xs.reshape(bs, 1, d),
            jnp.stack(sc_p), jnp.stack(sc_s),
            jnp.stack(k_p), jnp.stack(v_p), jnp.stack(lf_p),
            jnp.stack(k_s), jnp.stack(v_s), jnp.stack(lf_s),
            jnp.stack(lh_p), jnp.stack(lh_s),
            jnp.stack(lc_p), jnp.stack(lc_s),
            jnp.stack(fc_p), jnp.stack(fc_s))
```

```python
import functools

import jax
import jax.numpy as jnp
from jax import lax
from jax.experimental import pallas as pl
from jax.experimental.pallas import tpu as pltpu

F32 = jnp.float32
BF16 = jnp.bfloat16

EPS = 1e-6
HEAD_DIM = 64
LRU_C = 8.0
MASK_VALUE = -1e30
LOG2E = 1.4426950408889634
VT_ROWS = HEAD_DIM + 16

SUBLANES = 8
LANES = 128
V7X_VMEM_LIMIT_BYTES = 56 * 1024 * 1024
ROW_TILE = 512
LRU_ROW_TILE = 256
DECODE_PAGES_PER_STEP = 16


def _dot(a, b):
    return jnp.dot(a, b, preferred_element_type=F32)


def _dot_nt(a, b):
    return lax.dot_general(a, b, (((1,), (1,)), ((), ())), preferred_element_type=F32)


def _split2(x):
    hi = x.astype(BF16)
    lo = (x - hi.astype(F32)).astype(BF16)
    return hi, lo


def _split3(x):
    hi = x.astype(BF16)
    r = x - hi.astype(F32)
    mid = r.astype(BF16)
    lo = (r - mid.astype(F32)).astype(BF16)
    return hi, mid, lo


def _dot_exact_rhs01(x_parts, m01):
    out = _dot(x_parts[0], m01)
    for p in x_parts[1:]:
        out = out + _dot(p, m01)
    return out


def _rmsnorm(x, g):
    ms = jnp.mean(x * x, axis=-1, keepdims=True)
    return x * lax.rsqrt(ms + EPS) * g


def _sigmoid(x):
    return 1.0 / (1.0 + jnp.exp2(x * (-LOG2E)))


def _softplus(x):
    return jnp.maximum(x, 0.0) + jnp.log1p(jnp.exp(-jnp.abs(x)))


def _gelu_tanh(x):
    k = -2.0 * LOG2E * 0.7978845608028654
    return x / (1.0 + jnp.exp2(x * (k + (k * 0.044715) * (x * x))))


def _resident(shape):
    n = len(shape)
    return pl.BlockSpec(shape, lambda *_: (0,) * n, pipeline_mode=pl.Buffered(1))


def _layer_spec(stacked, layer):
    tail = stacked.shape[1:]
    zeros = (0,) * len(tail)
    return pl.BlockSpec((None,) + tail, lambda *_: (layer,) + zeros, pipeline_mode=pl.Buffered(1))


def _split_layered(operands):
    arrays, specs = [], []
    for op in operands:
        if isinstance(op, tuple):
            stacked, layer = op
            if stacked.ndim == 2:
                stacked = stacked[:, None, :]
            arrays.append(stacked)
            specs.append(_layer_spec(stacked, layer))
        else:
            arrays.append(op)
            specs.append(_resident(op.shape))
    return arrays, specs


def _params(*sem):
    return pltpu.CompilerParams(dimension_semantics=sem, vmem_limit_bytes=V7X_VMEM_LIMIT_BYTES)


def _row_tile(t, pref):
    return pref if t % pref == 0 else t


def _ffn_chunks(f):
    for n in (1,):
        if f % (n * LANES) == 0:
            return n
    return 1


def _head_rmsnorm(x, gain_tiled, seg, expand):
    ss = _dot_exact_rhs01(_split2(x * x), seg)
    inv = lax.rsqrt(ss * (1.0 / HEAD_DIM) + EPS)
    return x * _dot_exact_rhs01(_split2(inv), expand) * gain_tiled


def _log_sigmoid(x):
    return -_softplus(-x)


def _lru_gates(y, wai_ref, ba, bi, lam, w):
    yb = y.astype(BF16)
    half = w // 2
    parts = [_dot(yb[:, k * half:(k + 1) * half], wai_ref[k]) for k in range(2)]
    r = _sigmoid(jnp.concatenate([pt[:, 0:half] for pt in parts], axis=1) + ba)
    i = _sigmoid(jnp.concatenate([pt[:, half:2 * half] for pt in parts], axis=1) + bi)
    a = jnp.exp2(r * ((-LRU_C * LOG2E) * _softplus(-lam)))
    b = jnp.sqrt(1.0 - a * a) * (i * y)
    return a, b


def _sconv_pre_kernel(x_ref, g_ref, win_ref, cw_ref, z_ref, nb_ref, ubuf):
    tm, d = x_ref.shape[1], x_ref.shape[2]

    @pl.when(pl.program_id(1) == 0)
    def _():
        ubuf[0:SUBLANES, :] = jnp.zeros((SUBLANES, d), F32)

    xn = _rmsnorm(x_ref[0], g_ref[...]).astype(BF16)
    u = _dot(xn, win_ref[:, d:2 * d]) * _dot(xn, win_ref[:, 2 * d:3 * d])
    ubuf[SUBLANES:SUBLANES + tm, :] = u
    y = (cw_ref[0:1, :] * ubuf[SUBLANES - 2:SUBLANES - 2 + tm, :]
         + cw_ref[1:2, :] * ubuf[SUBLANES - 1:SUBLANES - 1 + tm, :]
         + cw_ref[2:3, :] * u)
    z_ref[0] = (_dot(xn, win_ref[:, 0:d]) * y).astype(BF16)
    nb_ref[0] = ubuf[SUBLANES + tm - 2:SUBLANES + tm, :]
    ubuf[0:SUBLANES, :] = ubuf[tm:tm + SUBLANES, :]


def _sconv_pre(x, weights):
    b, t, d = x.shape
    tm = _row_tile(t, ROW_TILE)
    arrays, specs = _split_layered(weights)
    return pl.pallas_call(
        _sconv_pre_kernel,
        grid=(b, t // tm),
        in_specs=[pl.BlockSpec((1, tm, d), lambda i, j: (i, j, 0))] + specs,
        out_specs=[
            pl.BlockSpec((1, tm, d), lambda i, j: (i, j, 0)),
            pl.BlockSpec((1, 2, d), lambda i, j: (i, 0, 0)),
        ],
        out_shape=[
            jax.ShapeDtypeStruct((b, t, d), BF16),
            jax.ShapeDtypeStruct((b, 2, d), F32),
        ],
        scratch_shapes=[pltpu.VMEM((SUBLANES + tm, d), F32)],
        compiler_params=_params("arbitrary", "arbitrary"),
        name="sconv_pre",
    )(x, *arrays)


def _ffn_body(x1, xn, gprev_fn, gstore_fn, wg_ref, wu_ref, cw_ref, cb_ref, wd_ref):
    f = wg_ref.shape[1]
    n_chunks = _ffn_chunks(f)
    fc = f // n_chunks
    acc = x1
    for c in range(n_chunks):
        sl = slice(c * fc, (c + 1) * fc)
        g = _dot(xn, wg_ref[:, sl])
        gstore_fn(sl, g)
        gc = (cw_ref[0:1, sl] * gprev_fn(sl, 2) + cw_ref[1:2, sl] * gprev_fn(sl, 1)
              + cw_ref[2:3, sl] * g + cb_ref[:, sl])
        up = _dot(xn, wu_ref[:, sl])
        hmid = (gc * _sigmoid(gc) * up).astype(BF16)
        acc = acc + _dot(hmid, wd_ref[sl, :])
    return acc


def _post_kernel(x_ref, z_ref, wo_ref, g_ref, wg_ref, wu_ref, cw_ref, cb_ref, wd_ref,
                 o_ref, nb_ref, gbuf):
    tm = x_ref.shape[1]
    f = wg_ref.shape[1]

    @pl.when(pl.program_id(1) == 0)
    def _():
        gbuf[0:SUBLANES, :] = jnp.zeros((SUBLANES, f), F32)

    x1 = x_ref[0] + _dot(z_ref[0], wo_ref[...])
    xn = _rmsnorm(x1, g_ref[...]).astype(BF16)

    def gstore(sl, g):
        gbuf[SUBLANES:SUBLANES + tm, sl] = g

    def gprev(sl, k):
        return gbuf[SUBLANES - k:SUBLANES - k + tm, sl]

    o_ref[0] = _ffn_body(x1, xn, gprev, gstore, wg_ref, wu_ref, cw_ref, cb_ref, wd_ref)
    nb_ref[0] = gbuf[SUBLANES + tm - 2:SUBLANES + tm, :]
    gbuf[0:SUBLANES, :] = gbuf[tm:tm + SUBLANES, :]


def _post(x, z, weights, f):
    b, t, d = x.shape
    kz = z.shape[2]
    tm = _row_tile(t, ROW_TILE)
    arrays, specs = _split_layered(weights)
    return pl.pallas_call(
        _post_kernel,
        grid=(b, t // tm),
        in_specs=[
            pl.BlockSpec((1, tm, d), lambda i, j: (i, j, 0)),
            pl.BlockSpec((1, tm, kz), lambda i, j: (i, j, 0)),
        ] + specs,
        out_specs=[
            pl.BlockSpec((1, tm, d), lambda i, j: (i, j, 0)),
            pl.BlockSpec((1, 2, f), lambda i, j: (i, 0, 0)),
        ],
        out_shape=[
            jax.ShapeDtypeStruct((b, t, d), F32),
            jax.ShapeDtypeStruct((b, 2, f), F32),
        ],
        scratch_shapes=[pltpu.VMEM((SUBLANES + tm, f), F32)],
        compiler_params=_params("arbitrary", "arbitrary"),
        name="post_ffn",
    )(x, z, *arrays)


def _fox_project(xn, wqkv_ref, wf_ref, bf_ref, qn_ref, kn_ref, seg_ref, exp_ref):
    d = xn.shape[1]
    q = _head_rmsnorm(_dot(xn, wqkv_ref[:, 0:d]), qn_ref[...], seg_ref[...], exp_ref[...])
    k = _head_rmsnorm(_dot(xn, wqkv_ref[:, d:2 * d]), kn_ref[...], seg_ref[...], exp_ref[...])
    v = _dot(xn, wqkv_ref[:, 2 * d:3 * d])
    lf = _log_sigmoid(_dot(xn, wf_ref[...]) + bf_ref[...])
    return q, k, v, lf


def _dot_exact_rhs01_lhs(m01, x_parts):
    out = _dot(m01, x_parts[0])
    for p in x_parts[1:]:
        out = out + _dot(m01, p)
    return out


def _fox_pre_kernel(x_ref, g_ref, wqkv_ref, wf_ref, bf_ref, qn_ref, kn_ref, seg_ref, exp_ref,
                    tri_ref, qa_ref, ka_ref, vt_ref, k_ref, v_ref, lf_ref, carry):
    tm = x_ref.shape[1]
    n_heads = lf_ref.shape[2]

    @pl.when(pl.program_id(1) == 0)
    def _():
        carry[...] = jnp.zeros_like(carry)

    xn = _rmsnorm(x_ref[0], g_ref[...]).astype(BF16)
    d = xn.shape[1]
    v = _dot(xn, wqkv_ref[:, 2 * d:3 * d])
    v_ref[0] = v
    vt_ref[0, :, 0, 0:HEAD_DIM, :] = v.T.reshape(n_heads, HEAD_DIM, tm).astype(BF16)
    pad_rows = vt_ref.shape[3] - HEAD_DIM
    first = lax.broadcasted_iota(jnp.int32, (n_heads, pad_rows, tm), 1) == 0
    vt_ref[0, :, 0, HEAD_DIM:HEAD_DIM + pad_rows, :] = jnp.where(first, 1.0, 0.0).astype(BF16)

    lf = _log_sigmoid(_dot(xn, wf_ref[...]) + bf_ref[...])
    lf_ref[0] = lf[:, 0:n_heads]
    c = _dot_exact_rhs01_lhs(tri_ref[...], _split3(lf)) + carry[0:1, :]
    carry[0:1, :] = c[tm - 1:tm, :]
    lane = lax.broadcasted_iota(jnp.int32, (tm, LANES), 1)
    lane_row = lax.broadcasted_iota(jnp.int32, (1, LANES), 1)
    hi, mid, lo = (part.astype(F32) for part in _split3(c * LOG2E))
    bias_lo = jnp.where(lane < n_heads, hi,
                        jnp.where(lane < 2 * n_heads, pltpu.roll(mid, n_heads, 1),
                                  jnp.where(lane < 3 * n_heads, pltpu.roll(lo, 2 * n_heads, 1), 0.0)))
    bias_hi = pltpu.roll(bias_lo, HEAD_DIM, 1)
    lower = lane < HEAD_DIM

    k = _head_rmsnorm(_dot(xn, wqkv_ref[:, d:2 * d]), kn_ref[...], seg_ref[...], exp_ref[...])
    k_ref[0] = k
    for h in range(n_heads):
        col = slice((h // 2) * LANES, (h // 2 + 1) * LANES)
        ka = jnp.where(lower, k[:, col], bias_hi) if h % 2 == 0 else jnp.where(lower, bias_lo, k[:, col])
        ka_ref[0, h] = ka.astype(BF16)

    q = _head_rmsnorm(_dot(xn, wqkv_ref[:, 0:d]), qn_ref[...], seg_ref[...], exp_ref[...])
    qs = q * (HEAD_DIM ** -0.5 * LOG2E)
    for h in range(n_heads):
        col = slice((h // 2) * LANES, (h // 2 + 1) * LANES)
        base = h + (HEAD_DIM if h % 2 == 0 else 0)
        pick = jnp.where(lane_row == base, -1.0, 0.0)
        pick = pick + jnp.where(lane_row == base + n_heads, -1.0, 0.0)
        pick = pick + jnp.where(lane_row == base + 2 * n_heads, -1.0, 0.0)
        qa = jnp.where(lower, qs[:, col], pick) if h % 2 == 0 else jnp.where(lower, pick, qs[:, col])
        qa_ref[0, h] = qa.astype(BF16)


def _fox_pre(x, weights, tm, n_heads):
    b, t, d = x.shape
    row = lambda i, j: (i, j, 0)
    head = lambda i, j: (i, 0, j, 0)
    arrays, specs = _split_layered(weights)
    return pl.pallas_call(
        _fox_pre_kernel,
        grid=(b, t // tm),
        in_specs=[pl.BlockSpec((1, tm, d), row)] + specs,
        out_specs=[
            pl.BlockSpec((1, n_heads, tm, LANES), head),
            pl.BlockSpec((1, n_heads, tm, LANES), head),
            pl.BlockSpec((1, n_heads, 1, VT_ROWS, tm), lambda i, j: (i, 0, j, 0, 0)),
            pl.BlockSpec((1, tm, d), row),
            pl.BlockSpec((1, tm, d), row),
            pl.BlockSpec((1, tm, n_heads), row),
        ],
        out_shape=[
            jax.ShapeDtypeStruct((b, n_heads, t, LANES), BF16),
            jax.ShapeDtypeStruct((b, n_heads, t, LANES), BF16),
            jax.ShapeDtypeStruct((b, n_heads, t // tm, VT_ROWS, tm), BF16),
            jax.ShapeDtypeStruct((b, t, d), F32),
            jax.ShapeDtypeStruct((b, t, d), F32),
            jax.ShapeDtypeStruct((b, t, n_heads), F32),
        ],
        scratch_shapes=[pltpu.VMEM((SUBLANES, LANES), F32)],
        compiler_params=_params("arbitrary", "arbitrary"),
        name="fox_pre",
    )(x, *arrays)


def _fox_attn_kernel(qa_ref, ka_ref, vt_ref, o_ref, acc_ref, st_a, st_b):
    pair, tq = qa_ref.shape[1], qa_ref.shape[2]
    vrows = vt_ref.shape[3]
    qi = pl.program_id(2)
    acc_ref[...] = jnp.zeros(acc_ref.shape, F32)

    def logits_into(st_ref, ki):
        start = pl.multiple_of(ki * tq, tq)
        for j in range(pair):
            st_ref[j] = _dot_nt(ka_ref[0, j, pl.ds(start, tq), :], qa_ref[0, j])

    def consume(st_ref, ki, m_prev, masked):
        m_out = []
        for j in range(pair):
            st = st_ref[j]
            if masked:
                key_idx = lax.broadcasted_iota(jnp.int32, (tq, tq), 0)
                qry_idx = lax.broadcasted_iota(jnp.int32, (tq, tq), 1)
                st = jnp.where(key_idx <= qry_idx, st, MASK_VALUE)
            m_new = jnp.maximum(m_prev[j], jnp.max(st, axis=0, keepdims=True))
            pt = jnp.exp2(st - m_new).astype(BF16)
            rows = slice(j * vrows, (j + 1) * vrows)
            acc_ref[rows, :] = jnp.exp2(m_prev[j] - m_new) * acc_ref[rows, :] + _dot(vt_ref[0, j, ki], pt)
            m_out.append(m_new)
        return tuple(m_out)

    odd = qi % 2
    m = tuple(jnp.full((1, tq), MASK_VALUE, F32) for _ in range(pair))

    def odd_start(m):
        logits_into(st_b, 0)
        logits_into(st_a, 1)
        return consume(st_b, 0, m, False)

    def even_start(m):
        logits_into(st_a, 0)
        return m

    m = lax.cond(odd == 1, odd_start, even_start, m)

    def two_blocks(i, m):
        k0 = odd + 2 * i
        logits_into(st_b, k0 + 1)
        m = consume(st_a, k0, m, False)
        logits_into(st_a, k0 + 2)
        return consume(st_b, k0 + 1, m, False)

    pairs = qi // 2
    lead = pairs % 2
    m = lax.cond(lead == 1, lambda m: two_blocks(0, m), lambda m: m, m)
    m = lax.fori_loop(0, pairs // 2,
                      lambda i, m: two_blocks(lead + 2 * i + 1, two_blocks(lead + 2 * i, m)), m)
    consume(st_a, qi, m, True)
    outs = []
    for j in range(pair):
        base = j * vrows
        outs.append(acc_ref[base:base + HEAD_DIM, :] / acc_ref[base + HEAD_DIM:base + HEAD_DIM + 1, :])
    o_ref[0] = jnp.concatenate(outs, axis=0).T.astype(BF16)


def _fox_attn(qa, ka, vt):
    b, n_heads, t, _ = qa.shape
    nk, tq = vt.shape[2], vt.shape[4]
    pair = 4
    return pl.pallas_call(
        _fox_attn_kernel,
        grid=(b, n_heads // pair, t // tq),
        in_specs=[
            pl.BlockSpec((1, pair, tq, LANES), lambda i, h, j: (i, h, j, 0)),
            pl.BlockSpec((1, pair, t, LANES), lambda i, h, j: (i, h, 0, 0)),
            pl.BlockSpec((1, pair, nk, vt.shape[3], tq), lambda i, h, j: (i, h, 0, 0, 0)),
        ],
        out_specs=pl.BlockSpec((1, tq, pair * HEAD_DIM), lambda i, h, j: (i, j, h)),
        out_shape=jax.ShapeDtypeStruct((b, t, n_heads * HEAD_DIM), BF16),
        scratch_shapes=[
            pltpu.VMEM((pair * vt.shape[3], tq), F32),
            pltpu.VMEM((pair, tq, tq), F32),
            pltpu.VMEM((pair, tq, tq), F32),
        ],
        compiler_params=_params("arbitrary", "arbitrary", "arbitrary"),
        name="fox_attn",
    )(qa, ka, vt)


def _scan_rows(a, b, h0):
    rows, w = a.shape
    groups = rows // SUBLANES
    a3 = a.reshape(groups, SUBLANES, w)
    b3 = b.reshape(groups, SUBLANES, w)
    sub = lax.broadcasted_iota(jnp.int32, (groups, SUBLANES, w), 1)
    s = 1
    while s < SUBLANES:
        keep = sub >= s
        b3 = a3 * jnp.where(keep, pltpu.roll(b3, s, 1), 0.0) + b3
        a3 = a3 * jnp.where(keep, pltpu.roll(a3, s, 1), 1.0)
        s *= 2
    last = SUBLANES - 1
    entry = [h0]
    for g in range(groups - 1):
        entry.append(a3[g, last:last + 1, :] * entry[g] + b3[g, last:last + 1, :])
    return jnp.concatenate([a3[g] * entry[g] + b3[g] for g in range(groups)], axis=0)


def _lru_pre_kernel(x_ref, g_ref, win_ref, bin_ref, cw_ref, cb_ref, wai_ref, ba_ref, bi_ref,
                    lam_ref, z_ref, nh_ref, nc_ref, xbuf, hcarry):
    tm = x_ref.shape[1]
    w = cw_ref.shape[1]
    taps = cw_ref.shape[0]

    @pl.when(pl.program_id(1) == 0)
    def _():
        xbuf[0:SUBLANES, :] = jnp.zeros((SUBLANES, w), F32)
        hcarry[...] = jnp.zeros_like(hcarry)

    xn = _rmsnorm(x_ref[0], g_ref[...]).astype(BF16)
    xb = _dot(xn, win_ref[:, w:2 * w]) + bin_ref[:, w:2 * w]
    xbuf[SUBLANES:SUBLANES + tm, :] = xb
    y = cw_ref[taps - 1:taps, :] * xb + cb_ref[...]
    for k in range(1, taps):
        y = y + cw_ref[taps - 1 - k:taps - k, :] * xbuf[SUBLANES - k:SUBLANES - k + tm, :]
    nc_ref[0] = xbuf[SUBLANES + tm - (taps - 1):SUBLANES + tm, :]
    xbuf[0:SUBLANES, :] = xbuf[tm:tm + SUBLANES, :]

    a, bt = _lru_gates(y, wai_ref, ba_ref[...], bi_ref[...], lam_ref[...], w)
    h = _scan_rows(a, bt, hcarry[0:1, :])
    hcarry[0:1, :] = h[tm - 1:tm, :]
    nh_ref[0] = h[tm - 1:tm, :]
    gate = _gelu_tanh(_dot(xn, win_ref[:, 0:w]) + bin_ref[:, 0:w])
    z_ref[0] = (h * gate).astype(BF16)


def _lru_pre(x, weights, w, taps):
    b, t, d = x.shape
    tm = _row_tile(t, LRU_ROW_TILE)
    arrays, specs = _split_layered(weights)
    return pl.pallas_call(
        _lru_pre_kernel,
        grid=(b, t // tm),
        in_specs=[pl.BlockSpec((1, tm, d), lambda i, j: (i, j, 0))] + specs,
        out_specs=[
            pl.BlockSpec((1, tm, w), lambda i, j: (i, j, 0)),
            pl.BlockSpec((1, 1, w), lambda i, j: (i, 0, 0)),
            pl.BlockSpec((1, taps - 1, w), lambda i, j: (i, 0, 0)),
        ],
        out_shape=[
            jax.ShapeDtypeStruct((b, t, w), BF16),
            jax.ShapeDtypeStruct((b, 1, w), F32),
            jax.ShapeDtypeStruct((b, taps - 1, w), F32),
        ],
        scratch_shapes=[
            pltpu.VMEM((SUBLANES + tm, w), F32),
            pltpu.VMEM((SUBLANES, w), F32),
        ],
        compiler_params=_params("arbitrary", "arbitrary"),
        name="lru_pre",
    )(x, *arrays)


def _whole(shape):
    n = len(shape)
    return pl.BlockSpec(shape, lambda *_: (0,) * n)


def _single_step_call(kernel, inputs, out_shapes, name):
    inputs, specs = _split_layered(inputs)
    return pl.pallas_call(
        kernel,
        grid=(1,),
        in_specs=specs,
        out_specs=[_whole(s.shape) for s in out_shapes],
        out_shape=out_shapes,
        compiler_params=_params("arbitrary"),
        name=name,
    )(*inputs)


def _sconv_s_kernel(x_ref, g_ref, win_ref, cw_ref, st_ref, z_ref, ns_ref):
    d = x_ref.shape[1]
    xn = _rmsnorm(x_ref[...], g_ref[...]).astype(BF16)
    bg = _dot(xn, win_ref[:, 0:d])
    u = _dot(xn, win_ref[:, d:2 * d]) * _dot(xn, win_ref[:, 2 * d:3 * d])
    b0, b1 = st_ref[:, 0, :], st_ref[:, 1, :]
    y = cw_ref[0:1, :] * b0 + cw_ref[1:2, :] * b1 + cw_ref[2:3, :] * u
    z_ref[...] = (bg * y).astype(BF16)
    ns_ref[:, 0, :] = b1
    ns_ref[:, 1, :] = u


def _post_s_kernel(x_ref, z_ref, wo_ref, g_ref, wg_ref, wu_ref, cw_ref, cb_ref, wd_ref,
                   st_ref, o_ref, ns_ref):
    x1 = x_ref[...] + _dot(z_ref[...], wo_ref[...])
    xn = _rmsnorm(x1, g_ref[...]).astype(BF16)
    ns_ref[:, 0, :] = st_ref[:, 1, :]

    def gstore(sl, g):
        ns_ref[:, 1, sl] = g

    def gprev(sl, k):
        return st_ref[:, 2 - k, sl]

    o_ref[...] = _ffn_body(x1, xn, gprev, gstore, wg_ref, wu_ref, cw_ref, cb_ref, wd_ref)


def _fox_pre_s_kernel(x_ref, g_ref, wqkv_ref, wf_ref, bf_ref, qn_ref, kn_ref, seg_ref, exp_ref,
                      q_ref, k_ref, v_ref, lf_ref):
    xn = _rmsnorm(x_ref[...], g_ref[...]).astype(BF16)
    q, k, v, lf = _fox_project(xn, wqkv_ref, wf_ref, bf_ref, qn_ref, kn_ref, seg_ref, exp_ref)
    q_ref[...] = q * (HEAD_DIM ** -0.5)
    k_ref[...] = k
    v_ref[...] = v
    lf_ref[...] = lf


def _lru_s_kernel(x_ref, g_ref, win_ref, bin_ref, cw_ref, cb_ref, wai_ref, ba_ref, bi_ref, lam_ref,
                  h0_ref, st_ref, z_ref, h_ref, ns_ref):
    w = cw_ref.shape[1]
    taps = cw_ref.shape[0]
    xn = _rmsnorm(x_ref[...], g_ref[...]).astype(BF16)
    gate = _gelu_tanh(_dot(xn, win_ref[:, 0:w]) + bin_ref[:, 0:w])
    xb = _dot(xn, win_ref[:, w:2 * w]) + bin_ref[:, w:2 * w]
    y = cw_ref[taps - 1:taps, :] * xb + cb_ref[...]
    for k in range(taps - 1):
        prev = st_ref[:, k, :]
        y = y + cw_ref[k:k + 1, :] * prev
        if k > 0:
            ns_ref[:, k - 1, :] = prev
    ns_ref[:, taps - 2, :] = xb
    a, bt = _lru_gates(y, wai_ref, ba_ref[...], bi_ref[...], lam_ref[...], w)
    h = a * h0_ref[...] + bt
    h_ref[...] = h
    z_ref[...] = (h * gate).astype(BF16)


def _decode_attn_kernel(pt_ref, *refs, n_slots):
    k_refs = refs[:n_slots]
    v_refs = refs[n_slots:2 * n_slots]
    lf_refs = refs[2 * n_slots:3 * n_slots]
    (q_ref, kn_ref, qb_ref, vnb_ref, lfn_ref, later_ref,
     o_ref, m_ref, l_ref, acc_ref, lf_after) = refs[3 * n_slots:]
    n_heads = q_ref.shape[1]

    @pl.when(pl.program_id(1) == 0)
    def _():
        lf_after[...] = jnp.zeros_like(lf_after)
        m_ref[...] = jnp.sum(q_ref[0] * kn_ref[0], axis=1, keepdims=True)
        l_ref[...] = jnp.ones_like(l_ref)
        lane = lax.broadcasted_iota(jnp.int32, acc_ref.shape[1:], 1)
        for h in range(n_heads):
            acc_ref[h] = jnp.where(lane == 0, vnb_ref[0, h], 0.0)

    logits = [None] * n_slots
    for i in reversed(range(n_slots)):
        lf = lf_refs[i][...]
        bias = _dot_exact_rhs01(_split3(lf), later_ref[...]) + lf_after[...] + lfn_ref[0]
        lf_after[...] = lf_after[...] + jnp.sum(lf, axis=1, keepdims=True)
        rows = [jnp.sum(k_refs[i][h] * qb_ref[0, h], axis=0, keepdims=True) for h in range(n_heads)]
        logits[i] = jnp.concatenate(rows, axis=0) + bias
    m_prev = m_ref[...]
    m_new = m_prev
    for s in logits:
        m_new = jnp.maximum(m_new, jnp.max(s, axis=1, keepdims=True))
    alpha = jnp.exp(m_prev - m_new)
    probs = [jnp.exp(s - m_new) for s in logits]
    l_new = alpha * l_ref[...]
    for p in probs:
        l_new = l_new + jnp.sum(p, axis=1, keepdims=True)
    l_ref[...] = l_new
    m_ref[...] = m_new
    for h in range(n_heads):
        a = acc_ref[h] * alpha[h:h + 1, :]
        for i in range(n_slots):
            a = a + probs[i][h:h + 1, :] * v_refs[i][h]
        acc_ref[h] = a

    @pl.when(pl.program_id(1) == pl.num_programs(1) - 1)
    def _():
        for h in range(n_heads):
            o_ref[0, h] = jnp.sum(acc_ref[h], axis=1, keepdims=True) / l_ref[h:h + 1, :]


def _decode_attn(k_t, v_t, logf_t, layer, page_table, q3, kn3, qb, vnb, lf_new, later):
    bs, n_pages = page_table.shape
    n_heads, page = k_t.shape[2], k_t.shape[4]
    g = DECODE_PAGES_PER_STEP if n_pages % DECODE_PAGES_PER_STEP == 0 else 1
    n_groups = n_pages // g

    def page_spec(i):
        def idx(b, p, pt):
            return (layer, pt[b, (n_groups - 1 - p) * g + i], 0, 0, 0)
        return pl.BlockSpec((None, None, n_heads, HEAD_DIM, page), idx)

    def logf_spec(i):
        def idx(b, p, pt):
            return (layer, pt[b, (n_groups - 1 - p) * g + i], 0, 0)
        return pl.BlockSpec((None, None, n_heads, page), idx)

    per_seq = pl.BlockSpec((1, n_heads, HEAD_DIM), lambda b, p, pt: (b, 0, 0))
    per_seq_b = pl.BlockSpec((1, n_heads, HEAD_DIM, page), lambda b, p, pt: (b, 0, 0, 0))
    grid_spec = pltpu.PrefetchScalarGridSpec(
        num_scalar_prefetch=1,
        grid=(bs, n_groups),
        in_specs=([page_spec(i) for i in range(g)] + [page_spec(i) for i in range(g)]
                  + [logf_spec(i) for i in range(g)]
                  + [per_seq, per_seq, per_seq_b, per_seq_b,
                     pl.BlockSpec((1, n_heads, 1), lambda b, p, pt: (b, 0, 0)),
                     pl.BlockSpec((page, page), lambda b, p, pt: (0, 0))]),
        out_specs=pl.BlockSpec((1, n_heads, HEAD_DIM, 1), lambda b, p, pt: (b, 0, 0, 0)),
        scratch_shapes=[
            pltpu.VMEM((n_heads, 1), F32),
            pltpu.VMEM((n_heads, 1), F32),
            pltpu.VMEM((n_heads, HEAD_DIM, page), F32),
            pltpu.VMEM((n_heads, 1), F32),
        ],
    )
    return pl.pallas_call(
        functools.partial(_decode_attn_kernel, n_slots=g),
        grid_spec=grid_spec,
        out_shape=jax.ShapeDtypeStruct((bs, n_heads, HEAD_DIM, 1), F32),
        compiler_params=_params("arbitrary", "arbitrary"),
        name="decode_attn",
    )(page_table, *([k_t] * g), *([v_t] * g), *([logf_t] * g), q3, kn3, qb, vnb, lf_new, later)


def _block_diag(w):
    n, c, _ = w.shape
    eye = jnp.eye(n, dtype=w.dtype)
    return (eye[:, None, :, None] * w[:, :, None, :]).reshape(n * c, n * c)


def kernel(x_prompt, x_sample, state_sconv, cache_k, cache_v, cache_logf, page_table, state_lru_h, state_lru_conv, state_ffn_conv, mix_norm, ffn_norm, sc_w_in, sc_conv_w, sc_w_out, fox_w_qkv, fox_w_f, fox_b_f, fox_q_norm, fox_k_norm, fox_w_o, lru_w_in, lru_b_in, lru_conv_w, lru_conv_b, lru_w_a, lru_b_a, lru_w_i, lru_b_i, lru_lambda, lru_w_out, ffn_w_gate, ffn_w_up, ffn_conv_w, ffn_conv_b, ffn_w_down):
    depth, d = mix_norm.shape
    bp, tp, _ = x_prompt.shape
    bs, ts, _ = x_sample.shape
    assert ts == 1, "the sample group carries one new row per sequence"
    n_heads = fox_w_f.shape[-1]
    assert n_heads * HEAD_DIM == d and n_heads % 2 == 0 and 3 * n_heads <= HEAD_DIM
    n_mixers = 3
    f = ffn_w_gate.shape[-1]
    page = cache_k.shape[2]

    xp = x_prompt
    xs = x_sample.reshape(bs, d)
    row2 = lambda v: v.reshape(1, -1)

    lane_head = jnp.arange(d, dtype=jnp.int32) // HEAD_DIM
    seg = (lane_head[:, None] == jnp.arange(LANES, dtype=jnp.int32)[None, :]).astype(BF16)
    expand = seg.T
    tm_fox = _row_tile(tp, ROW_TILE)
    tri = (jnp.arange(tm_fox)[:, None] >= jnp.arange(tm_fox)[None, :]).astype(BF16)
    later = (jnp.arange(page)[:, None] > jnp.arange(page)[None, :]).astype(BF16)
    k_t = jnp.transpose(cache_k, (0, 1, 3, 4, 2))
    v_t = jnp.transpose(cache_v, (0, 1, 3, 4, 2))
    logf_t = jnp.transpose(cache_logf, (0, 1, 3, 2))

    bf = lambda w: w.astype(BF16)
    sc_w_in_b, sc_w_out_b = bf(sc_w_in), bf(sc_w_out)
    fox_w_qkv_b, fox_w_o_b = bf(fox_w_qkv), bf(fox_w_o)
    lru_w_in_b, lru_w_out_b = bf(lru_w_in), bf(lru_w_out)
    ffn_w_gate_b, ffn_w_up_b, ffn_w_down_b = bf(ffn_w_gate), bf(ffn_w_up), bf(ffn_w_down)

    sc_p, sc_s, fc_p, fc_s = [], [], [], []
    k_p, v_p, lf_p, k_s, v_s, lf_s = [], [], [], [], [], []
    lh_p, lh_s, lc_p, lc_s = [], [], [], []

    for i in range(depth):
        j = i // n_mixers
        g_mix = (mix_norm, i)
        if i % n_mixers == 0:
            weights = [g_mix, (sc_w_in_b, j), (sc_conv_w, j)]
            w_out = (sc_w_out_b, j)
            zp, nbp = _sconv_pre(xp, weights)
            zs, us = _single_step_call(
                _sconv_s_kernel,
                [xs, *weights, (state_sconv, j)],
                [jax.ShapeDtypeStruct((bs, d), BF16), jax.ShapeDtypeStruct(state_sconv.shape[1:], F32)],
                "sconv_sample")
            sc_p.append(nbp)
            sc_s.append(us)
        elif i % n_mixers == 1:
            w_f = jnp.pad(fox_w_f[j], ((0, 0), (0, LANES - n_heads))).astype(BF16)
            b_f = jnp.pad(fox_b_f[j], (0, LANES - n_heads)).reshape(1, LANES)
            qn = row2(jnp.tile(fox_q_norm[j], n_heads))
            kn = row2(jnp.tile(fox_k_norm[j], n_heads))
            weights = [g_mix, (fox_w_qkv_b, j), w_f, b_f, qn, kn, seg, expand]
            w_out = (fox_w_o_b, j)
            qa, ka, vt, kp, vp, lfp = _fox_pre(xp, weights + [tri], tm_fox, n_heads)
            zp = _fox_attn(qa, ka, vt)
            k_p.append(kp.reshape(bp, tp, n_heads, HEAD_DIM))
            v_p.append(vp.reshape(bp, tp, n_heads, HEAD_DIM))
            lf_p.append(lfp)

            qs, ks, vs, lfs = _single_step_call(
                _fox_pre_s_kernel,
                [xs, *weights],
                [jax.ShapeDtypeStruct((bs, d), F32)] * 3 + [jax.ShapeDtypeStruct((bs, LANES), F32)],
                "fox_pre_sample")
            lfs = lfs[:, :n_heads]
            qs3 = qs.reshape(bs, n_heads, HEAD_DIM)
            ks3 = ks.reshape(bs, n_heads, HEAD_DIM)
            vs3 = vs.reshape(bs, n_heads, HEAD_DIM)
            along_lanes = lambda a: jnp.broadcast_to(a[..., None], (bs, n_heads, HEAD_DIM, page))
            os4 = _decode_attn(k_t, v_t, logf_t, j, page_table, qs3, ks3, along_lanes(qs3), along_lanes(vs3),
                               lfs.reshape(bs, n_heads, 1), later)
            zs = os4.reshape(bs, d).astype(BF16)
            k_s.append(ks3.reshape(bs, 1, n_heads, HEAD_DIM))
            v_s.append(vs3.reshape(bs, 1, n_heads, HEAD_DIM))
            lf_s.append(lfs.reshape(bs, 1, n_heads))
        else:
            w = lru_conv_w.shape[-1]
            taps = lru_conv_w.shape[1]
            nb = lru_w_a.shape[1]
            assert nb % 2 == 0 and (w // 2) % LANES == 0
            w_ai = jnp.stack([
                jnp.concatenate([_block_diag(lru_w_a[j, k * nb // 2:(k + 1) * nb // 2]),
                                 _block_diag(lru_w_i[j, k * nb // 2:(k + 1) * nb // 2])], axis=1)
                for k in range(2)]).astype(BF16)
            weights = [g_mix, (lru_w_in_b, j), (lru_b_in, j), (lru_conv_w, j), (lru_conv_b, j), w_ai,
                       (lru_b_a, j), (lru_b_i, j), (lru_lambda, j)]
            w_out = (lru_w_out_b, j)
            zp, nhp, ncp = _lru_pre(xp, weights, w, taps)
            zs, nhs, xbs = _single_step_call(
                _lru_s_kernel,
                [xs, *weights, (state_lru_h, j), (state_lru_conv, j)],
                [jax.ShapeDtypeStruct((bs, w), BF16), jax.ShapeDtypeStruct((bs, w), F32),
                 jax.ShapeDtypeStruct(state_lru_conv.shape[1:], F32)],
                "lru_sample")
            lh_p.append(nhp.reshape(bp, w))
            lh_s.append(nhs)
            lc_p.append(ncp)
            lc_s.append(xbs)

        post_w = [w_out, (ffn_norm, i), (ffn_w_gate_b, i), (ffn_w_up_b, i), (ffn_conv_w, i), (ffn_conv_b, i),
                  (ffn_w_down_b, i)]
        xp, nfp = _post(xp, zp, post_w, f)
        xs, gs = _single_step_call(
            _post_s_kernel,
            [xs, zs, *post_w, (state_ffn_conv, i)],
            [jax.ShapeDtypeStruct((bs, d), F32), jax.ShapeDtypeStruct(state_ffn_conv.shape[1:], F32)],
            "post_ffn_sample")
        fc_p.append(nfp)
        fc_s.append(gs)

    return (xp, xs.reshape(bs, 1, d),
            jnp.stack(sc_p), jnp.stack(sc_s),
            jnp.stack(k_p), jnp.stack(v_p), jnp.stack(lf_p),
            jnp.stack(k_s), jnp.stack(v_s), jnp.stack(lf_s),
            jnp.stack(lh_p), jnp.stack(lh_s),
            jnp.stack(lc_p), jnp.stack(lc_s),
            jnp.stack(fc_p), jnp.stack(fc_s))
```
